```python
import math
import jax, jax.numpy as jnp
from jax import lax
import numpy as np

D_MODEL = 2048
BATCH = 1
SEQ = 16384
DEPTH = 1
DEC_BATCH = 4
DEC_SEQ = 4096
PAST_LEN = 128

MLA_HEADS = 8
MLA_Q_LORA = 768
MLA_KV_LORA = 512
MLA_NOPE = 128
MLA_ROPE = 64
MLA_V = 128
DIFF_HEADS = 8
DIFF_QK = 64
DIFF_V = 2 * DIFF_QK
DIFF_ROT = DIFF_QK // 4
D_FF = 4 * D_MODEL
ROPE_THETA = 500000.0
Q_BLOCK = 128
LN_EPS = 1e-5
RMS_EPS = 1e-6
DN_ALPHA = (2.0 * DEPTH) ** 0.25
DN_BETA = (8.0 * DEPTH) ** -0.25

C_QA = MLA_Q_LORA
C_KVA = MLA_KV_LORA + MLA_ROPE
C_DQ = DIFF_HEADS * 2 * DIFF_QK
C_DK = DIFF_HEADS * 2 * DIFF_QK
C_DV = DIFF_HEADS * DIFF_V
C_GATE = 2 * D_MODEL
C_IN = C_QA + C_KVA + C_DQ + C_DK + C_DV + C_GATE
IN_SPLITS = (C_QA, C_QA + C_KVA, C_QA + C_KVA + C_DQ, C_QA + C_KVA + C_DQ + C_DK,
             C_QA + C_KVA + C_DQ + C_DK + C_DV)

kernel_name = 'hybrid_mla_diffattn_gated_encoder'


def _layernorm(x, g, b):
    xf = x.astype(jnp.float32)
    mu = jnp.mean(xf, -1, keepdims=True)
    var = jnp.mean(jnp.square(xf - mu), -1, keepdims=True)
    y = (xf - mu) * lax.rsqrt(var + LN_EPS) * g.astype(jnp.float32) + b.astype(jnp.float32)
    return y.astype(x.dtype)


def _rmsnorm(x, g):
    xf = x.astype(jnp.float32)
    ms = jnp.mean(jnp.square(xf), -1, keepdims=True)
    return (xf * lax.rsqrt(ms + RMS_EPS) * g.astype(jnp.float32)).astype(x.dtype)


def _rope_tables(seq_len, rot_dim):
    inv_freq = ROPE_THETA ** (-jnp.arange(0, rot_dim, 2, dtype=jnp.float32) / rot_dim)
    ang = jnp.arange(seq_len, dtype=jnp.float32)[:, None] * inv_freq[None, :]
    return jnp.cos(ang), jnp.sin(ang)


def _rope(x, cos, sin):
    half = x.shape[-1] // 2
    shp = (x.shape[1],) + (1,) * (x.ndim - 3) + (half,)
    c, s = cos.reshape(shp), sin.reshape(shp)
    xf = x.astype(jnp.float32)
    x1, x2 = xf[..., :half], xf[..., half:]
    return jnp.concatenate([x1 * c - x2 * s, x2 * c + x1 * s], axis=-1).astype(x.dtype)


def _partial_rope(x, cos, sin):
    return jnp.concatenate([_rope(x[..., :DIFF_ROT], cos, sin), x[..., DIFF_ROT:]], axis=-1)


def _blocks(t):
    b, s = t.shape[0], t.shape[1]
    return jnp.moveaxis(t.reshape((b, s // Q_BLOCK, Q_BLOCK) + t.shape[2:]), 1, 0)


def _unblocks(t):
    nb, b, q = t.shape[0], t.shape[1], t.shape[2]
    return jnp.moveaxis(t, 0, 1).reshape((b, nb * q) + t.shape[3:])


def _mla_attention(q_nope, q_rope, k_nope, k_rope, v):
    scale = (MLA_NOPE + MLA_ROPE) ** -0.5

    def block(qs):
        qn, qr = qs
        s = (jnp.einsum('bqhd,bkhd->bhqk', qn, k_nope)
             + jnp.einsum('bqhr,bkr->bhqk', qr, k_rope)).astype(jnp.float32) * scale
        p = jax.nn.softmax(s, axis=-1)
        return jnp.einsum('bhqk,bkhd->bqhd', p.astype(v.dtype), v)

    return _unblocks(lax.map(block, (_blocks(q_nope), _blocks(q_rope))))


def _diff_attention(q, k, v, lam):
    scale = DIFF_QK ** -0.5

    def block(qb):
        s = jnp.einsum('bqhmd,bkhmd->bhmqk', qb, k).astype(jnp.float32) * scale
        p = jax.nn.softmax(s, axis=-1)
        a = p[:, :, 0] - lam * p[:, :, 1]
        return jnp.einsum('bhqk,bkhd->bqhd', a.astype(v.dtype), v)

    return _unblocks(lax.map(block, _blocks(q)))


def _mixer(x, w_in, b_gate, g_qa, w_qb, g_kva, w_kvb, lam_q, lam_k, g_sub,
           w_br_mla, w_br_diff, w_out, lam_init):
    B, S, _ = x.shape
    proj = x @ w_in
    qa, kva, dq, dk, dv, gl = jnp.split(proj, IN_SPLITS, axis=-1)

    cos_m, sin_m = _rope_tables(S, MLA_ROPE)
    q = (_rmsnorm(qa, g_qa) @ w_qb).reshape(B, S, MLA_HEADS, MLA_NOPE + MLA_ROPE)
    q_nope = q[..., :MLA_NOPE]
    q_rope = _rope(q[..., MLA_NOPE:], cos_m, sin_m)
    c_kv = _rmsnorm(kva[..., :MLA_KV_LORA], g_kva)
    k_rope = _rope(kva[..., MLA_KV_LORA:], cos_m, sin_m)
    kv = (c_kv @ w_kvb).reshape(B, S, MLA_HEADS, MLA_NOPE + MLA_V)
    k_nope, v_mla = kv[..., :MLA_NOPE], kv[..., MLA_NOPE:]
    o_mla = _mla_attention(q_nope, q_rope, k_nope, k_rope, v_mla).reshape(B, S, MLA_HEADS * MLA_V)

    cos_d, sin_d = _rope_tables(S, DIFF_ROT)
    dq = _partial_rope(dq.reshape(B, S, DIFF_HEADS, 2, DIFF_QK), cos_d, sin_d)
    dk = _partial_rope(dk.reshape(B, S, DIFF_HEADS, 2, DIFF_QK), cos_d, sin_d)
    dv = dv.reshape(B, S, DIFF_HEADS, DIFF_V)
    lam_dot = jnp.sum(lam_q.astype(jnp.float32) * lam_k.astype(jnp.float32), axis=-1)
    lam = jnp.exp(lam_dot[0]) - jnp.exp(lam_dot[1]) + lam_init
    o_diff = _diff_attention(dq, dk, dv, lam)
    o_diff = (_rmsnorm(o_diff, g_sub) * (1.0 - lam_init)).reshape(B, S, DIFF_HEADS * DIFF_V)

    g_logit = gl + b_gate
    g_mla = jax.nn.sigmoid(g_logit[..., :D_MODEL])
    g_diff = jax.nn.sigmoid(g_logit[..., D_MODEL:])
    merged = g_mla * (o_mla @ w_br_mla) + g_diff * (o_diff @ w_br_diff)
    return merged @ w_out


def _trunk(x, w_in, b_gate, g_qa, w_qb, g_kva, w_kvb, lam_q, lam_k, g_sub,
           w_br_mla, w_br_diff, w_out, ln1_g, ln1_b, w_ff1, w_ff2, ln2_g, ln2_b):
    for l in range(DEPTH):
        lam_init = 0.8 - 0.6 * math.exp(-0.3 * l)
        h = _mixer(x, w_in[l], b_gate[l], g_qa[l], w_qb[l], g_kva[l], w_kvb[l],
                   lam_q[l], lam_k[l], g_sub[l], w_br_mla[l], w_br_diff[l], w_out[l], lam_init)
        x = _layernorm(DN_ALPHA * x + h, ln1_g[l], ln1_b[l])
        f = jnp.square(jax.nn.relu(x @ w_ff1[l])) @ w_ff2[l]
        x = _layernorm(DN_ALPHA * x + f, ln2_g[l], ln2_b[l])
    return x


def setup_inputs(seed: int = 0) -> dict:
    key = jax.random.key(seed)
    ks = jax.random.split(key, 24)

    def nrm(k, shape, scale):
        return jax.random.normal(k, shape, dtype=jnp.float32) * scale

    def gain(k, n):
        return 1.0 + nrm(k, (DEPTH, n), 0.02)

    return {
        'x_prompt': nrm(ks[0], (BATCH, SEQ, D_MODEL), 1.0),
        'x_sample': nrm(ks[1], (DEC_BATCH, DEC_SEQ, D_MODEL), 1.0),
        'w_in': nrm(ks[2], (DEPTH, D_MODEL, C_IN), D_MODEL ** -0.5),
        'b_gate': nrm(ks[3], (DEPTH, C_GATE), 0.01),
        'g_qa': gain(ks[4], MLA_Q_LORA),
        'w_qb': nrm(ks[5], (DEPTH, MLA_Q_LORA, MLA_HEADS * (MLA_NOPE + MLA_ROPE)), MLA_Q_LORA ** -0.5),
        'g_kva': gain(ks[6], MLA_KV_LORA),
        'w_kvb': nrm(ks[7], (DEPTH, MLA_KV_LORA, MLA_HEADS * (MLA_NOPE + MLA_V)), MLA_KV_LORA ** -0.5),
        'lam_q': nrm(ks[8], (DEPTH, 2, DIFF_QK), 0.1),
        'lam_k': nrm(ks[9], (DEPTH, 2, DIFF_QK), 0.1),
        'g_sub': gain(ks[10], DIFF_V),
        'w_br_mla': nrm(ks[11], (DEPTH, MLA_HEADS * MLA_V, D_MODEL), (MLA_HEADS * MLA_V) ** -0.5 * DN_BETA),
        'w_br_diff': nrm(ks[12], (DEPTH, DIFF_HEADS * DIFF_V, D_MODEL), (DIFF_HEADS * DIFF_V) ** -0.5 * DN_BETA),
        'w_out': nrm(ks[13], (DEPTH, D_MODEL, D_MODEL), D_MODEL ** -0.5 * DN_BETA),
        'ln1_g': gain(ks[14], D_MODEL),
        'ln1_b': nrm(ks[15], (DEPTH, D_MODEL), 0.02),
        'w_ff1': nrm(ks[16], (DEPTH, D_MODEL, D_FF), D_MODEL ** -0.5),
        'w_ff2': nrm(ks[17], (DEPTH, D_FF, D_MODEL), D_FF ** -0.5 * DN_BETA),
        'ln2_g': gain(ks[18], D_MODEL),
        'ln2_b': nrm(ks[19], (DEPTH, D_MODEL), 0.02),
    }


def reference(x_prompt, x_sample, w_in, b_gate, g_qa, w_qb, g_kva, w_kvb, lam_q, lam_k, g_sub,
              w_br_mla, w_br_diff, w_out, ln1_g, ln1_b, w_ff1, w_ff2, ln2_g, ln2_b):
    y_prompt = _trunk(x_prompt, w_in, b_gate, g_qa, w_qb, g_kva, w_kvb, lam_q, lam_k, g_sub,
                      w_br_mla, w_br_diff, w_out, ln1_g, ln1_b, w_ff1, w_ff2, ln2_g, ln2_b)
    y_sample = _trunk(x_sample, w_in, b_gate, g_qa, w_qb, g_kva, w_kvb, lam_q, lam_k, g_sub,
                      w_br_mla, w_br_diff, w_out, ln1_g, ln1_b, w_ff1, w_ff2, ln2_g, ln2_b)
    return (y_prompt, y_sample)
```

```python
import functools
import math

import jax
import jax.numpy as jnp
from jax import lax
from jax.experimental import pallas as pl
from jax.experimental.pallas import tpu as pltpu

D_MODEL = 2048
DEPTH = 1
MLA_HEADS = 8
MLA_Q_LORA = 768
MLA_KV_LORA = 512
MLA_NOPE = 128
MLA_ROPE = 64
MLA_V = 128
DIFF_HEADS = 8
DIFF_QK = 64
DIFF_V = 2 * DIFF_QK
DIFF_ROT = DIFF_QK // 4
D_FF = 4 * D_MODEL
ROPE_THETA = 500000.0
LN_EPS = 1e-5
RMS_EPS = 1e-6
DN_ALPHA = (2.0 * DEPTH) ** 0.25

C_QA = MLA_Q_LORA
C_KVA = MLA_KV_LORA + MLA_ROPE
C_DQ = DIFF_HEADS * 2 * DIFF_QK
C_DK = DIFF_HEADS * 2 * DIFF_QK
C_DV = DIFF_HEADS * DIFF_V

LANES = 128
HEAD_PAD = 2 * LANES
KEY_CHUNK = 512
MLA_BQ = 512
DIFF_BQ = 256
VMEM_LIMIT = 56 * 1024 * 1024

F32 = jnp.float32
BF16 = jnp.bfloat16


def _params(sem):
    return pltpu.CompilerParams(dimension_semantics=sem, vmem_limit_bytes=VMEM_LIMIT)


def _rope_mix(x, tab_ref, shift):
    n = x.shape[-1]
    fwd = pltpu.roll(x, n - shift, 1)
    bwd = pltpu.roll(x, shift, 1)
    return x * tab_ref[0] + fwd * tab_ref[1] + bwd * tab_ref[2]


def _rmsnorm_rows(x, g):
    ms = jnp.mean(x * x, axis=-1, keepdims=True)
    return x * lax.rsqrt(ms + RMS_EPS) * g


def _layernorm_rows(z, g, b):
    mu = jnp.mean(z, axis=-1, keepdims=True)
    zc = z - mu
    var = jnp.mean(zc * zc, axis=-1, keepdims=True)
    return zc * lax.rsqrt(var + LN_EPS) * g + b


def _q_proj_kernel(x_ref, wqa_ref, gqa_ref, wqb_ref, tab_ref, q_ref, *, scale):
    xb = x_ref[...].astype(BF16)
    qa = jnp.dot(xb, wqa_ref[...], preferred_element_type=F32)
    qn = _rmsnorm_rows(qa, gqa_ref[...]).astype(BF16)
    q = jnp.dot(qn, wqb_ref[...], preferred_element_type=F32)
    nope_w = MLA_HEADS * MLA_NOPE
    for h in range(MLA_HEADS):
        nope = q[:, h * MLA_NOPE:(h + 1) * MLA_NOPE]
        rope = _rope_mix(q[:, nope_w + h * LANES: nope_w + (h + 1) * LANES], tab_ref, MLA_ROPE // 2)
        q_ref[:, h * HEAD_PAD: h * HEAD_PAD + LANES] = (nope * scale).astype(BF16)
        q_ref[:, h * HEAD_PAD + LANES:(h + 1) * HEAD_PAD] = (rope * scale).astype(BF16)


def _kv_proj_kernel(x_ref, wkva_ref, gkva_ref, wkvb_ref, tab_ref, k_ref, vt_ref):
    xb = x_ref[...].astype(BF16)
    kva = jnp.dot(xb, wkva_ref[...], preferred_element_type=F32)
    ckv = _rmsnorm_rows(kva[:, :MLA_KV_LORA], gkva_ref[...]).astype(BF16)
    krope = _rope_mix(kva[:, MLA_KV_LORA:], tab_ref, MLA_ROPE // 2).astype(BF16)
    kv = jnp.dot(ckv, wkvb_ref[...], preferred_element_type=F32)
    nope_w = MLA_HEADS * MLA_NOPE
    for h in range(MLA_HEADS):
        k_ref[:, h * HEAD_PAD: h * HEAD_PAD + LANES] = kv[:, h * MLA_NOPE:(h + 1) * MLA_NOPE].astype(BF16)
        k_ref[:, h * HEAD_PAD + LANES:(h + 1) * HEAD_PAD] = krope
    vt = kv[:, nope_w:].T
    vt_ref[:, 0] = vt.reshape(MLA_HEADS, MLA_V, vt.shape[-1]).astype(BF16)


def _dq_proj_kernel(x_ref, w_ref, tab_ref, q1_ref, q2_ref, *, scale):
    xb = x_ref[...].astype(BF16)
    dq = jnp.dot(xb, w_ref[...], preferred_element_type=F32)
    first_map = lax.broadcasted_iota(jnp.int32, (dq.shape[0], LANES), 1) < DIFF_QK
    for h in range(DIFF_HEADS):
        r = _rope_mix(dq[:, h * LANES:(h + 1) * LANES], tab_ref, DIFF_ROT // 2) * scale
        q1_ref[:, h * LANES:(h + 1) * LANES] = jnp.where(first_map, r, 0.0).astype(BF16)
        q2_ref[:, h * LANES:(h + 1) * LANES] = jnp.where(first_map, 0.0, r).astype(BF16)


def _dk_proj_kernel(x_ref, w_ref, tab_ref, k_ref):
    xb = x_ref[...].astype(BF16)
    dk = jnp.dot(xb, w_ref[...], preferred_element_type=F32)
    for h in range(DIFF_HEADS):
        k_ref[:, h * LANES:(h + 1) * LANES] = _rope_mix(
            dk[:, h * LANES:(h + 1) * LANES], tab_ref, DIFF_ROT // 2).astype(BF16)


def _dv_proj_kernel(x_ref, w_ref, vt_ref):
    xb = x_ref[...].astype(BF16)
    dv = jnp.dot(xb, w_ref[...], preferred_element_type=F32)
    vt = dv.T
    vt_ref[:, 0] = vt.reshape(DIFF_HEADS, DIFF_V, vt.shape[-1]).astype(BF16)


def _gate_kernel(x_ref, w_ref, b_ref, g_ref):
    xb = x_ref[...].astype(BF16)
    z = jnp.dot(xb, w_ref[...], preferred_element_type=F32) + b_ref[...]
    g_ref[...] = 1.0 / (1.0 + jnp.exp(-z))


def _row_spec(tm, width):
    return pl.BlockSpec((tm, width), lambda i: (i, 0))


def _full_spec(shape):
    nd = len(shape)
    return pl.BlockSpec(shape, lambda i: (0,) * nd)


def _tab_spec(tm, s_tiles):
    return pl.BlockSpec((3, tm, LANES), lambda i: (0, i % s_tiles, 0))


def _vt_out(t, tm):
    shape = jax.ShapeDtypeStruct((MLA_HEADS, t // tm, MLA_V, tm), BF16)
    spec = pl.BlockSpec((MLA_HEADS, 1, MLA_V, tm), lambda i: (0, i, 0, 0))
    return shape, spec


def _q_proj(x, wqa, gqa, wqb, tab, seq):
    t, tm = x.shape[0], 512
    return pl.pallas_call(
        functools.partial(_q_proj_kernel, scale=(MLA_NOPE + MLA_ROPE) ** -0.5),
        grid=(t // tm,),
        in_specs=[_row_spec(tm, D_MODEL), _full_spec(wqa.shape), _full_spec(gqa.shape),
                  _full_spec(wqb.shape), _tab_spec(tm, seq // tm)],
        out_specs=_row_spec(tm, MLA_HEADS * HEAD_PAD),
        out_shape=jax.ShapeDtypeStruct((t, MLA_HEADS * HEAD_PAD), BF16),
        compiler_params=_params(("parallel",)),
        name="mla_q_proj",
    )(x, wqa, gqa, wqb, tab)


def _kv_proj(x, wkva, gkva, wkvb, tab, seq):
    t, tm = x.shape[0], KEY_CHUNK
    vt_shape, vt_spec = _vt_out(t, tm)
    return pl.pallas_call(
        _kv_proj_kernel,
        grid=(t // tm,),
        in_specs=[_row_spec(tm, D_MODEL), _full_spec(wkva.shape), _full_spec(gkva.shape),
                  _full_spec(wkvb.shape), _tab_spec(tm, seq // tm)],
        out_specs=[_row_spec(tm, MLA_HEADS * HEAD_PAD), vt_spec],
        out_shape=[jax.ShapeDtypeStruct((t, MLA_HEADS * HEAD_PAD), BF16), vt_shape],
        compiler_params=_params(("parallel",)),
        name="mla_kv_proj",
    )(x, wkva, gkva, wkvb, tab)


def _dq_proj(x, w, tab, seq):
    t, tm = x.shape[0], 512
    out = jax.ShapeDtypeStruct((t, C_DQ), BF16)
    return pl.pallas_call(
        functools.partial(_dq_proj_kernel, scale=DIFF_QK ** -0.5),
        grid=(t // tm,),
        in_specs=[_row_spec(tm, D_MODEL), _full_spec(w.shape), _tab_spec(tm, seq // tm)],
        out_specs=[_row_spec(tm, C_DQ), _row_spec(tm, C_DQ)],
        out_shape=[out, out],
        compiler_params=_params(("parallel",)),
        name="diff_q_proj",
    )(x, w, tab)


def _dk_proj(x, w, tab, seq):
    t, tm = x.shape[0], 512
    return pl.pallas_call(
        _dk_proj_kernel,
        grid=(t // tm,),
        in_specs=[_row_spec(tm, D_MODEL), _full_spec(w.shape), _tab_spec(tm, seq // tm)],
        out_specs=_row_spec(tm, C_DK),
        out_shape=jax.ShapeDtypeStruct((t, C_DK), BF16),
        compiler_params=_params(("parallel",)),
        name="diff_k_proj",
    )(x, w, tab)


def _dv_proj(x, w):
    t, tm = x.shape[0], KEY_CHUNK
    vt_shape, vt_spec = _vt_out(t, tm)
    return pl.pallas_call(
        _dv_proj_kernel,
        grid=(t // tm,),
        in_specs=[_row_spec(tm, D_MODEL), _full_spec(w.shape)],
        out_specs=vt_spec,
        out_shape=vt_shape,
        compiler_params=_params(("parallel",)),
        name="diff_v_proj",
    )(x, w)


def _gates(x, w, b):
    t, tm = x.shape[0], 512
    n = w.shape[1]
    tn = D_MODEL
    return pl.pallas_call(
        _gate_kernel,
        grid=(n // tn, t // tm),
        in_specs=[pl.BlockSpec((tm, D_MODEL), lambda j, i: (i, 0)),
                  pl.BlockSpec((D_MODEL, tn), lambda j, i: (0, j)),
                  pl.BlockSpec((1, tn), lambda j, i: (0, j))],
        out_specs=pl.BlockSpec((tm, tn), lambda j, i: (i, j)),
        out_shape=jax.ShapeDtypeStruct((t, n), F32),
        compiler_params=_params(("parallel", "parallel")),
        name="gates",
    )(x, w, b)


def _flash_loop(q, k_ref, vt_ref, acc_ref):
    nc = q.shape[0]
    n_chunks = vt_ref.shape[0]
    acc_ref[...] = jnp.zeros_like(acc_ref)

    def body(i, carry):
        m, l = carry
        start = pl.multiple_of(i * KEY_CHUNK, KEY_CHUNK)
        kc = k_ref[pl.ds(start, KEY_CHUNK), :]
        s = lax.dot_general(kc, q, (((1,), (1,)), ((), ())), preferred_element_type=F32)
        m_new = jnp.maximum(m, jnp.max(s, axis=0, keepdims=True))
        alpha = jnp.exp(m - m_new)
        p = jnp.exp(s - m_new)
        l_new = alpha * l + jnp.sum(p, axis=0, keepdims=True)
        pv = jnp.dot(vt_ref[i], p.astype(BF16), preferred_element_type=F32)
        acc_ref[...] = alpha * acc_ref[...] + pv
        return m_new, l_new

    m0 = jnp.full((1, nc), -jnp.inf, F32)
    l0 = jnp.zeros((1, nc), F32)
    _, l = lax.fori_loop(0, n_chunks, body, (m0, l0))
    return l


def _mla_flash_kernel(q_ref, k_ref, vt_ref, o_ref, acc_ref):
    l = _flash_loop(q_ref[...], k_ref, vt_ref, acc_ref)
    o_ref[...] = (acc_ref[...] * (1.0 / l)).T.astype(BF16)


def _diff_flash_kernel(q1_ref, q2_ref, k_ref, vt_ref, lq_ref, lk_ref, gsub_ref, o_ref, acc_ref, *, lam_init):
    bq = q1_ref.shape[0]
    q = jnp.concatenate([q1_ref[...], q2_ref[...]], axis=0)
    l = _flash_loop(q, k_ref, vt_ref, acc_ref)
    lam_dot = jnp.sum(lq_ref[...] * lk_ref[...], axis=-1, keepdims=True)
    lam_exp = jnp.exp(lam_dot)
    lam = lam_exp[0:1, :] - lam_exp[1:2, :] + lam_init
    on = acc_ref[...] * (1.0 / l)
    a = on[:, :bq] - lam * on[:, bq:]
    ms = jnp.mean(a * a, axis=0, keepdims=True)
    y = (a * lax.rsqrt(ms + RMS_EPS)).T * gsub_ref[...]
    o_ref[...] = (y * (1.0 - lam_init)).astype(BF16)


def _mla_flash(q, k, vt, batch, seq):
    t = q.shape[0]
    nq = seq // MLA_BQ
    n_chunks = seq // KEY_CHUNK
    return pl.pallas_call(
        _mla_flash_kernel,
        grid=(batch, MLA_HEADS, nq),
        in_specs=[pl.BlockSpec((MLA_BQ, HEAD_PAD), lambda b, h, i: (b * nq + i, h)),
                  pl.BlockSpec((seq, HEAD_PAD), lambda b, h, i: (b, h)),
                  pl.BlockSpec((None, n_chunks, MLA_V, KEY_CHUNK), lambda b, h, i: (h, b, 0, 0))],
        out_specs=pl.BlockSpec((MLA_BQ, MLA_V), lambda b, h, i: (b * nq + i, h)),
        out_shape=jax.ShapeDtypeStruct((t, MLA_HEADS * MLA_V), BF16),
        scratch_shapes=[pltpu.VMEM((MLA_V, MLA_BQ), F32)],
        compiler_params=_params(("parallel", "parallel", "parallel")),
        name="mla_flash",
    )(q, k, vt)


def _diff_flash(q1, q2, k, vt, lam_q, lam_k, g_sub, batch, seq, lam_init):
    t = q1.shape[0]
    nq = seq // DIFF_BQ
    n_chunks = seq // KEY_CHUNK
    q_spec = pl.BlockSpec((DIFF_BQ, LANES), lambda b, h, i: (b * nq + i, h))
    small = lambda shape: pl.BlockSpec(shape, lambda b, h, i: (0, 0))
    return pl.pallas_call(
        functools.partial(_diff_flash_kernel, lam_init=lam_init),
        grid=(batch, DIFF_HEADS, nq),
        in_specs=[q_spec, q_spec,
                  pl.BlockSpec((seq, LANES), lambda b, h, i: (b, h)),
                  pl.BlockSpec((None, n_chunks, DIFF_V, KEY_CHUNK), lambda b, h, i: (h, b, 0, 0)),
                  small(lam_q.shape), small(lam_k.shape), small(g_sub.shape)],
        out_specs=pl.BlockSpec((DIFF_BQ, DIFF_V), lambda b, h, i: (b * nq + i, h)),
        out_shape=jax.ShapeDtypeStruct((t, DIFF_HEADS * DIFF_V), BF16),
        scratch_shapes=[pltpu.VMEM((DIFF_V, 2 * DIFF_BQ), F32)],
        compiler_params=_params(("parallel", "parallel", "parallel")),
        name="diff_flash",
    )(q1, q2, k, vt, lam_q, lam_k, g_sub)


def _merge_kernel(om_ref, od_ref, gm_ref, gd_ref, wm_ref, wd_ref, o_ref):
    a = jnp.dot(om_ref[...], wm_ref[...], preferred_element_type=F32)
    b = jnp.dot(od_ref[...], wd_ref[...], preferred_element_type=F32)
    o_ref[...] = (gm_ref[...] * a + gd_ref[...] * b).astype(BF16)


def _outproj_ln_kernel(x_ref, m_ref, w_ref, g_ref, b_ref, o_ref):
    h = jnp.dot(m_ref[...], w_ref[...], preferred_element_type=F32)
    o_ref[...] = _layernorm_rows(DN_ALPHA * x_ref[...] + h, g_ref[...], b_ref[...])


def _ffn_ln_kernel(x_ref, w1_ref, w2_ref, g_ref, b_ref, o_ref, xb_ref, acc_ref):
    j = pl.program_id(1)

    @pl.when(j == 0)
    def _():
        xb_ref[...] = x_ref[...].astype(BF16)
        acc_ref[...] = jnp.zeros_like(acc_ref)

    h = jnp.maximum(jnp.dot(xb_ref[...], w1_ref[...], preferred_element_type=F32), 0.0)
    acc_ref[...] += jnp.dot((h * h).astype(BF16), w2_ref[...], preferred_element_type=F32)

    @pl.when(j == pl.num_programs(1) - 1)
    def _():
        o_ref[...] = _layernorm_rows(DN_ALPHA * x_ref[...] + acc_ref[...], g_ref[...], b_ref[...])


def _merge(o_mla, o_diff, gates, w_br_mla, w_br_diff):
    t, tm = o_mla.shape[0], 256
    return pl.pallas_call(
        _merge_kernel,
        grid=(t // tm,),
        in_specs=[_row_spec(tm, o_mla.shape[1]), _row_spec(tm, o_diff.shape[1]),
                  pl.BlockSpec((tm, D_MODEL), lambda i: (i, 0)),
                  pl.BlockSpec((tm, D_MODEL), lambda i: (i, 1)),
                  _full_spec(w_br_mla.shape), _full_spec(w_br_diff.shape)],
        out_specs=_row_spec(tm, D_MODEL),
        out_shape=jax.ShapeDtypeStruct((t, D_MODEL), BF16),
        compiler_params=_params(("parallel",)),
        name="branch_merge",
    )(o_mla, o_diff, gates, gates, w_br_mla, w_br_diff)


def _outproj_ln(x, merged, w_out, g, b):
    t, tm = x.shape[0], 256
    return pl.pallas_call(
        _outproj_ln_kernel,
        grid=(t // tm,),
        in_specs=[_row_spec(tm, D_MODEL), _row_spec(tm, D_MODEL), _full_spec(w_out.shape),
                  _full_spec(g.shape), _full_spec(b.shape)],
        out_specs=_row_spec(tm, D_MODEL),
        out_shape=jax.ShapeDtypeStruct((t, D_MODEL), F32),
        compiler_params=_params(("parallel",)),
        name="outproj_ln",
    )(x, merged, w_out, g, b)


def _ffn_ln(x, w1, w2, g, b):
    t, tm, tf = x.shape[0], 512, 1024
    return pl.pallas_call(
        _ffn_ln_kernel,
        grid=(t // tm, D_FF // tf),
        in_specs=[pl.BlockSpec((tm, D_MODEL), lambda i, j: (i, 0)),
                  pl.BlockSpec((D_MODEL, tf), lambda i, j: (0, j)),
                  pl.BlockSpec((tf, D_MODEL), lambda i, j: (j, 0)),
                  pl.BlockSpec((1, D_MODEL), lambda i, j: (0, 0)),
                  pl.BlockSpec((1, D_MODEL), lambda i, j: (0, 0))],
        out_specs=pl.BlockSpec((tm, D_MODEL), lambda i, j: (i, 0)),
        out_shape=jax.ShapeDtypeStruct((t, D_MODEL), F32),
        scratch_shapes=[pltpu.VMEM((tm, D_MODEL), BF16), pltpu.VMEM((tm, D_MODEL), F32)],
        compiler_params=_params(("parallel", "arbitrary")),
        name="ffn_ln",
    )(x, w1, w2, g, b)


def _rope_tables(seq, rot_dim, group):
    half = rot_dim // 2
    inv_freq = ROPE_THETA ** (-jnp.arange(0, rot_dim, 2, dtype=F32) / rot_dim)
    ang = jnp.arange(seq, dtype=F32)[:, None] * inv_freq[None, :]
    cos, sin = jnp.cos(ang), jnp.sin(ang)
    zeros = lambda n: jnp.zeros((seq, n), F32)
    c = jnp.concatenate([cos, cos, jnp.ones((seq, group - rot_dim), F32)], axis=1)
    s_fwd = jnp.concatenate([-sin, zeros(group - half)], axis=1)
    s_bwd = jnp.concatenate([zeros(half), sin, zeros(group - rot_dim)], axis=1)
    return jnp.stack([c, s_fwd, s_bwd])


def _prep_weights(w_in, b_gate, g_qa, w_qb, g_kva, w_kvb, g_sub, w_br_mla, w_br_diff, w_out,
                  ln1_g, ln1_b, w_ff1, w_ff2, ln2_g, ln2_b, l):
    c0 = C_QA
    c1 = c0 + C_KVA
    c2 = c1 + C_DQ
    c3 = c2 + C_DK
    c4 = c3 + C_DV
    wi = w_in[l]
    qb = w_qb[l].reshape(MLA_Q_LORA, MLA_HEADS, MLA_NOPE + MLA_ROPE)
    qb_nope = qb[:, :, :MLA_NOPE].reshape(MLA_Q_LORA, MLA_HEADS * MLA_NOPE)
    qb_rope = jnp.pad(qb[:, :, MLA_NOPE:], ((0, 0), (0, 0), (0, LANES - MLA_ROPE)))
    qb_rope = qb_rope.reshape(MLA_Q_LORA, MLA_HEADS * LANES)
    kvb = w_kvb[l].reshape(MLA_KV_LORA, MLA_HEADS, MLA_NOPE + MLA_V)
    kvb_k = kvb[:, :, :MLA_NOPE].reshape(MLA_KV_LORA, MLA_HEADS * MLA_NOPE)
    kvb_v = kvb[:, :, MLA_NOPE:].reshape(MLA_KV_LORA, MLA_HEADS * MLA_V)
    row = lambda v: v[l].reshape(1, -1)
    return dict(
        wqa=wi[:, :c0].astype(BF16),
        wkva=jnp.pad(wi[:, c0:c1], ((0, 0), (0, LANES - MLA_ROPE))).astype(BF16),
        wdq=wi[:, c1:c2].astype(BF16),
        wdk=wi[:, c2:c3].astype(BF16),
        wdv=wi[:, c3:c4].astype(BF16),
        wgate=wi[:, c4:].astype(BF16),
        b_gate=row(b_gate),
        g_qa=row(g_qa),
        wqb=jnp.concatenate([qb_nope, qb_rope], axis=1).astype(BF16),
        g_kva=row(g_kva),
        wkvb=jnp.concatenate([kvb_k, kvb_v], axis=1).astype(BF16),
        g_sub=row(g_sub),
        w_br_mla=w_br_mla[l].astype(BF16),
        w_br_diff=w_br_diff[l].astype(BF16),
        w_out=w_out[l].astype(BF16),
        ln1_g=row(ln1_g), ln1_b=row(ln1_b),
        w_ff1=w_ff1[l].astype(BF16), w_ff2=w_ff2[l].astype(BF16),
        ln2_g=row(ln2_g), ln2_b=row(ln2_b),
    )


def _layer(x3, w, lam_q, lam_k, lam_init):
    batch, seq, _ = x3.shape
    x = x3.reshape(batch * seq, D_MODEL)
    tab_mla = jnp.pad(_rope_tables(seq, MLA_ROPE, MLA_ROPE), ((0, 0), (0, 0), (0, LANES - MLA_ROPE)))
    tab_diff = jnp.tile(_rope_tables(seq, DIFF_ROT, DIFF_QK), (1, 1, LANES // DIFF_QK))

    q = _q_proj(x, w["wqa"], w["g_qa"], w["wqb"], tab_mla, seq)
    k, vt = _kv_proj(x, w["wkva"], w["g_kva"], w["wkvb"], tab_mla, seq)
    o_mla = _mla_flash(q, k, vt, batch, seq)

    q1, q2 = _dq_proj(x, w["wdq"], tab_diff, seq)
    dk = _dk_proj(x, w["wdk"], tab_diff, seq)
    dvt = _dv_proj(x, w["wdv"])
    o_diff = _diff_flash(q1, q2, dk, dvt, lam_q, lam_k, w["g_sub"], batch, seq, lam_init)

    gates = _gates(x, w["wgate"], w["b_gate"])
    merged = _merge(o_mla, o_diff, gates, w["w_br_mla"], w["w_br_diff"])
    x1 = _outproj_ln(x, merged, w["w_out"], w["ln1_g"], w["ln1_b"])
    y = _ffn_ln(x1, w["w_ff1"], w["w_ff2"], w["ln2_g"], w["ln2_b"])
    return y.reshape(batch, seq, D_MODEL)


def kernel(x_prompt, x_sample, w_in, b_gate, g_qa, w_qb, g_kva, w_kvb, lam_q, lam_k, g_sub,
           w_br_mla, w_br_diff, w_out, ln1_g, ln1_b, w_ff1, w_ff2, ln2_g, ln2_b):
    outs = [x_prompt, x_sample]
    for l in range(DEPTH):
        lam_init = 0.8 - 0.6 * math.exp(-0.3 * l)
        w = _prep_weights(w_in, b_gate, g_qa, w_qb, g_kva, w_kvb, g_sub, w_br_mla, w_br_diff, w_out,
                          ln1_g, ln1_b, w_ff1, w_ff2, ln2_g, ln2_b, l)
        outs = [_layer(x3, w, lam_q[l], lam_k[l], lam_init) for x3 in outs]
    return tuple(outs)
```

```python
import functools
import math

import jax
import jax.numpy as jnp
from jax import lax
from jax.experimental import pallas as pl
from jax.experimental.pallas import tpu as pltpu

D_MODEL = 2048
DEPTH = 1
MLA_HEADS = 8
MLA_Q_LORA = 768
MLA_KV_LORA = 512
MLA_NOPE = 128
MLA_ROPE = 64
MLA_V = 128
DIFF_HEADS = 8
DIFF_QK = 64
DIFF_V = 2 * DIFF_QK
DIFF_ROT = DIFF_QK // 4
D_FF = 4 * D_MODEL
ROPE_THETA = 500000.0
LN_EPS = 1e-5
RMS_EPS = 1e-6
DN_ALPHA = (2.0 * DEPTH) ** 0.25
LOG2_E = math.log2(math.e)

C_QA = MLA_Q_LORA
C_KVA = MLA_KV_LORA + MLA_ROPE
C_DQ = DIFF_HEADS * 2 * DIFF_QK
C_DK = DIFF_HEADS * 2 * DIFF_QK
C_DV = DIFF_HEADS * DIFF_V

LANES = 128
HEAD_PAD = 2 * LANES
KEY_CHUNK = 1024
KEY_TILE = 256
SUBLANES = 8
VT_TM = 512
MLA_BQ = 512
DIFF_BQ = 256
VMEM_LIMIT = 56 * 1024 * 1024

F32 = jnp.float32
BF16 = jnp.bfloat16


def _params(sem):
    return pltpu.CompilerParams(dimension_semantics=sem, vmem_limit_bytes=VMEM_LIMIT)


def _rope_mix(x, tab_ref, shift):
    n = x.shape[-1]
    fwd = pltpu.roll(x, n - shift, 1)
    bwd = pltpu.roll(x, shift, 1)
    return x * tab_ref[0] + fwd * tab_ref[1] + bwd * tab_ref[2]


def _rmsnorm_rows(x, g):
    ms = jnp.mean(x * x, axis=-1, keepdims=True)
    return x * lax.rsqrt(ms + RMS_EPS) * g


def _layernorm_rows(z, g, b):
    mu = jnp.mean(z, axis=-1, keepdims=True)
    zc = z - mu
    var = jnp.mean(zc * zc, axis=-1, keepdims=True)
    return zc * lax.rsqrt(var + LN_EPS) * g + b


def _q_proj_kernel(x_ref, wqa_ref, gqa_ref, wqb_ref, tab_ref, q_ref, *, scale):
    xb = x_ref[...].astype(BF16)
    qa = jnp.dot(xb, wqa_ref[...], preferred_element_type=F32)
    qn = _rmsnorm_rows(qa, gqa_ref[...]).astype(BF16)
    q = jnp.dot(qn, wqb_ref[...], preferred_element_type=F32)
    nope_w = MLA_HEADS * MLA_NOPE
    for h in range(MLA_HEADS):
        nope = q[:, h * MLA_NOPE:(h + 1) * MLA_NOPE]
        rope = _rope_mix(q[:, nope_w + h * LANES: nope_w + (h + 1) * LANES], tab_ref, MLA_ROPE // 2)
        q_ref[:, h * HEAD_PAD: h * HEAD_PAD + LANES] = (nope * scale).astype(BF16)
        q_ref[:, h * HEAD_PAD + LANES:(h + 1) * HEAD_PAD] = (rope * scale).astype(BF16)


def _kv_proj_kernel(x_ref, wkva_ref, gkva_ref, wkvb_ref, tab_ref, k_ref, vt_ref):
    xb = x_ref[...].astype(BF16)
    kva = jnp.dot(xb, wkva_ref[...], preferred_element_type=F32)
    ckv = _rmsnorm_rows(kva[:, :MLA_KV_LORA], gkva_ref[...]).astype(BF16)
    krope = _rope_mix(kva[:, MLA_KV_LORA:], tab_ref, MLA_ROPE // 2).astype(BF16)
    kv = jnp.dot(ckv, wkvb_ref[...], preferred_element_type=F32)
    nope_w = MLA_HEADS * MLA_NOPE
    for h in range(MLA_HEADS):
        k_ref[:, h * HEAD_PAD: h * HEAD_PAD + LANES] = kv[:, h * MLA_NOPE:(h + 1) * MLA_NOPE].astype(BF16)
        k_ref[:, h * HEAD_PAD + LANES:(h + 1) * HEAD_PAD] = krope
    vt = kv[:, nope_w:].T
    vt_ref[:, 0] = vt.reshape(MLA_HEADS, MLA_V, vt.shape[-1]).astype(BF16)


def _dq_proj_kernel(x_ref, w_ref, tab_ref, q1_ref, q2_ref, *, scale):
    xb = x_ref[...].astype(BF16)
    dq = jnp.dot(xb, w_ref[...], preferred_element_type=F32)
    first_map = lax.broadcasted_iota(jnp.int32, (dq.shape[0], LANES), 1) < DIFF_QK
    for h in range(DIFF_HEADS):
        r = _rope_mix(dq[:, h * LANES:(h + 1) * LANES], tab_ref, DIFF_ROT // 2) * scale
        q1_ref[:, h * LANES:(h + 1) * LANES] = jnp.where(first_map, r, 0.0).astype(BF16)
        q2_ref[:, h * LANES:(h + 1) * LANES] = jnp.where(first_map, 0.0, r).astype(BF16)


def _dk_proj_kernel(x_ref, w_ref, tab_ref, k_ref):
    xb = x_ref[...].astype(BF16)
    dk = jnp.dot(xb, w_ref[...], preferred_element_type=F32)
    for h in range(DIFF_HEADS):
        k_ref[:, h * LANES:(h + 1) * LANES] = _rope_mix(
            dk[:, h * LANES:(h + 1) * LANES], tab_ref, DIFF_ROT // 2).astype(BF16)


def _dv_proj_kernel(x_ref, w_ref, vt_ref):
    xb = x_ref[...].astype(BF16)
    dv = jnp.dot(xb, w_ref[...], preferred_element_type=F32)
    vt = dv.T
    vt_ref[:, 0] = vt.reshape(DIFF_HEADS, DIFF_V, vt.shape[-1]).astype(BF16)


def _gate_kernel(x_ref, w_ref, b_ref, g_ref):
    xb = x_ref[...].astype(BF16)
    z = jnp.dot(xb, w_ref[...], preferred_element_type=F32) + b_ref[...]
    g_ref[...] = (1.0 / (1.0 + jnp.exp(-z))).astype(g_ref.dtype)


def _row_spec(tm, width):
    return pl.BlockSpec((tm, width), lambda i: (i, 0))


def _full_spec(shape):
    nd = len(shape)
    return pl.BlockSpec(shape, lambda i: (0,) * nd)


def _tab_spec(tm, s_tiles):
    return pl.BlockSpec((3, tm, LANES), lambda i: (0, i % s_tiles, 0))


def _vt_out(t, tm):
    per_chunk = KEY_CHUNK // tm
    shape = jax.ShapeDtypeStruct((MLA_HEADS, t // KEY_CHUNK, MLA_V, KEY_CHUNK), BF16)
    spec = pl.BlockSpec((MLA_HEADS, 1, MLA_V, tm), lambda i: (0, i // per_chunk, 0, i % per_chunk))
    return shape, spec


def _q_proj(x, wqa, gqa, wqb, tab, seq):
    t, tm = x.shape[0], 512
    return pl.pallas_call(
        functools.partial(_q_proj_kernel, scale=LOG2_E * (MLA_NOPE + MLA_ROPE) ** -0.5),
        grid=(t // tm,),
        in_specs=[_row_spec(tm, D_MODEL), _full_spec(wqa.shape), _full_spec(gqa.shape),
                  _full_spec(wqb.shape), _tab_spec(tm, seq // tm)],
        out_specs=_row_spec(tm, MLA_HEADS * HEAD_PAD),
        out_shape=jax.ShapeDtypeStruct((t, MLA_HEADS * HEAD_PAD), BF16),
        compiler_params=_params(("parallel",)),
        name="mla_q_proj",
    )(x, wqa, gqa, wqb, tab)


def _kv_proj(x, wkva, gkva, wkvb, tab, seq):
    t, tm = x.shape[0], VT_TM
    vt_shape, vt_spec = _vt_out(t, tm)
    return pl.pallas_call(
        _kv_proj_kernel,
        grid=(t // tm,),
        in_specs=[_row_spec(tm, D_MODEL), _full_spec(wkva.shape), _full_spec(gkva.shape),
                  _full_spec(wkvb.shape), _tab_spec(tm, seq // tm)],
        out_specs=[_row_spec(tm, MLA_HEADS * HEAD_PAD), vt_spec],
        out_shape=[jax.ShapeDtypeStruct((t, MLA_HEADS * HEAD_PAD), BF16), vt_shape],
        compiler_params=_params(("parallel",)),
        name="mla_kv_proj",
    )(x, wkva, gkva, wkvb, tab)


def _dq_proj(x, w, tab, seq):
    t, tm = x.shape[0], 512
    out = jax.ShapeDtypeStruct((t, C_DQ), BF16)
    return pl.pallas_call(
        functools.partial(_dq_proj_kernel, scale=LOG2_E * DIFF_QK ** -0.5),
        grid=(t // tm,),
        in_specs=[_row_spec(tm, D_MODEL), _full_spec(w.shape), _tab_spec(tm, seq // tm)],
        out_specs=[_row_spec(tm, C_DQ), _row_spec(tm, C_DQ)],
        out_shape=[out, out],
        compiler_params=_params(("parallel",)),
        name="diff_q_proj",
    )(x, w, tab)


def _dk_proj(x, w, tab, seq):
    t, tm = x.shape[0], 512
    return pl.pallas_call(
        _dk_proj_kernel,
        grid=(t // tm,),
        in_specs=[_row_spec(tm, D_MODEL), _full_spec(w.shape), _tab_spec(tm, seq // tm)],
        out_specs=_row_spec(tm, C_DK),
        out_shape=jax.ShapeDtypeStruct((t, C_DK), BF16),
        compiler_params=_params(("parallel",)),
        name="diff_k_proj",
    )(x, w, tab)


def _dv_proj(x, w):
    t, tm = x.shape[0], VT_TM
    vt_shape, vt_spec = _vt_out(t, tm)
    return pl.pallas_call(
        _dv_proj_kernel,
        grid=(t // tm,),
        in_specs=[_row_spec(tm, D_MODEL), _full_spec(w.shape)],
        out_specs=vt_spec,
        out_shape=vt_shape,
        compiler_params=_params(("parallel",)),
        name="diff_v_proj",
    )(x, w)


def _gates(x, w, b):
    t, tm = x.shape[0], 512
    n = w.shape[1]
    tn = D_MODEL
    return pl.pallas_call(
        _gate_kernel,
        grid=(n // tn, t // tm),
        in_specs=[pl.BlockSpec((tm, D_MODEL), lambda j, i: (i, 0)),
                  pl.BlockSpec((D_MODEL, tn), lambda j, i: (0, j)),
                  pl.BlockSpec((1, tn), lambda j, i: (0, j))],
        out_specs=pl.BlockSpec((tm, tn), lambda j, i: (i, j)),
        out_shape=jax.ShapeDtypeStruct((t, n), BF16),
        compiler_params=_params(("parallel", "parallel")),
        name="gates",
    )(x, w, b)


def _flash_loop(q, k_ref, vt_ref, s_ref, p_ref, acc_ref):
    nc = q.shape[0]
    n_chunks = vt_ref.shape[0]
    n_tiles = KEY_CHUNK // KEY_TILE
    q_t = q.astype(F32).T.astype(BF16)

    def sublane_groups(x):
        return x.reshape(KEY_TILE // SUBLANES, SUBLANES, nc)

    def score_tile(i, t):
        start = pl.multiple_of(i * KEY_CHUNK + t * KEY_TILE, KEY_TILE)
        return jnp.dot(k_ref[pl.ds(start, KEY_TILE), :], q_t, preferred_element_type=F32)

    def step(i, carry):
        m, l, alpha_prev, s_max = carry
        m_new = jnp.maximum(m, s_max)
        alpha = jnp.exp2(m - m_new)
        i_next = jnp.minimum(i + 1, n_chunks - 1)
        i_prev = jnp.maximum(i - 1, 0)
        pv = jnp.dot(vt_ref[i_prev], p_ref[...], preferred_element_type=F32)
        acc_ref[...] = alpha_prev * acc_ref[...] + pv
        next_max = None
        p_sum = None
        for t in range(n_tiles):
            rows = pl.ds(t * KEY_TILE, KEY_TILE)
            p = jnp.exp2(s_ref[rows, :] - m_new)
            p_ref[rows, :] = p.astype(BF16)
            tile_sum = jnp.sum(sublane_groups(p), axis=0)
            p_sum = tile_sum if t == 0 else p_sum + tile_sum
            s_tile = score_tile(i_next, t)
            s_ref[rows, :] = s_tile
            tile_max = jnp.max(sublane_groups(s_tile), axis=0)
            next_max = tile_max if t == 0 else jnp.maximum(next_max, tile_max)
        l_new = alpha * l + jnp.sum(p_sum, axis=0, keepdims=True)
        return m_new, l_new, alpha, jnp.max(next_max, axis=0, keepdims=True)

    first_max = None
    for t in range(n_tiles):
        s_tile = score_tile(0, t)
        s_ref[pl.ds(t * KEY_TILE, KEY_TILE), :] = s_tile
        tile_max = jnp.max(sublane_groups(s_tile), axis=0)
        first_max = tile_max if t == 0 else jnp.maximum(first_max, tile_max)
    p_ref[...] = jnp.zeros(p_ref.shape, BF16)
    acc_ref[...] = jnp.zeros_like(acc_ref)

    init = (jnp.full((1, nc), -jnp.inf, F32), jnp.zeros((1, nc), F32), jnp.ones((1, nc), F32),
            jnp.max(first_max, axis=0, keepdims=True))
    _, l, alpha_last, _ = lax.fori_loop(0, n_chunks, step, init, unroll=2)
    pv = jnp.dot(vt_ref[n_chunks - 1], p_ref[...], preferred_element_type=F32)
    return alpha_last * acc_ref[...] + pv, l


def _mla_flash_kernel(q_ref, k_ref, vt_ref, o_ref, s_ref, p_ref, acc_ref):
    acc, l = _flash_loop(q_ref[...], k_ref, vt_ref, s_ref, p_ref, acc_ref)
    o_ref[...] = (acc * (1.0 / l)).T.astype(BF16)


def _diff_flash_kernel(q1_ref, q2_ref, k_ref, vt_ref, lq_ref, lk_ref, gsub_ref, o_ref,
                       s_ref, p_ref, acc_ref, *, lam_init):
    bq = q1_ref.shape[0]
    q = jnp.concatenate([q1_ref[...], q2_ref[...]], axis=0)
    acc, l = _flash_loop(q, k_ref, vt_ref, s_ref, p_ref, acc_ref)
    lam_dot = jnp.sum(lq_ref[...] * lk_ref[...], axis=-1, keepdims=True)
    lam_exp = jnp.exp(lam_dot)
    lam = lam_exp[0:1, :] - lam_exp[1:2, :] + lam_init
    on = acc * (1.0 / l)
    a = on[:, :bq] - lam * on[:, bq:]
    ms = jnp.mean(a * a, axis=0, keepdims=True)
    y = (a * lax.rsqrt(ms + RMS_EPS)).T * gsub_ref[...]
    o_ref[...] = (y * (1.0 - lam_init)).astype(BF16)


def _flash_scratch(dv, nc):
    return [pltpu.VMEM((KEY_CHUNK, nc), F32), pltpu.VMEM((KEY_CHUNK, nc), BF16),
            pltpu.VMEM((dv, nc), F32)]


def _mla_flash(q, k, vt, batch, seq):
    t = q.shape[0]
    nq = seq // MLA_BQ
    n_chunks = seq // KEY_CHUNK
    return pl.pallas_call(
        _mla_flash_kernel,
        grid=(batch, MLA_HEADS, nq),
        in_specs=[pl.BlockSpec((MLA_BQ, HEAD_PAD), lambda b, h, i: (b * nq + i, h)),
                  pl.BlockSpec((seq, HEAD_PAD), lambda b, h, i: (b, h)),
                  pl.BlockSpec((None, n_chunks, MLA_V, KEY_CHUNK), lambda b, h, i: (h, b, 0, 0))],
        out_specs=pl.BlockSpec((MLA_BQ, MLA_V), lambda b, h, i: (b * nq + i, h)),
        out_shape=jax.ShapeDtypeStruct((t, MLA_HEADS * MLA_V), BF16),
        scratch_shapes=_flash_scratch(MLA_V, MLA_BQ),
        compiler_params=_params(("parallel", "parallel", "parallel")),
        name="mla_flash",
    )(q, k, vt)


def _diff_flash(q1, q2, k, vt, lam_q, lam_k, g_sub, batch, seq, lam_init):
    t = q1.shape[0]
    nq = seq // DIFF_BQ
    n_chunks = seq // KEY_CHUNK
    q_spec = pl.BlockSpec((DIFF_BQ, LANES), lambda b, h, i: (b * nq + i, h))
    small = lambda shape: pl.BlockSpec(shape, lambda b, h, i: (0, 0))
    return pl.pallas_call(
        functools.partial(_diff_flash_kernel, lam_init=lam_init),
        grid=(batch, DIFF_HEADS, nq),
        in_specs=[q_spec, q_spec,
                  pl.BlockSpec((seq, LANES), lambda b, h, i: (b, h)),
                  pl.BlockSpec((None, n_chunks, DIFF_V, KEY_CHUNK), lambda b, h, i: (h, b, 0, 0)),
                  small(lam_q.shape), small(lam_k.shape), small(g_sub.shape)],
        out_specs=pl.BlockSpec((DIFF_BQ, DIFF_V), lambda b, h, i: (b * nq + i, h)),
        out_shape=jax.ShapeDtypeStruct((t, DIFF_HEADS * DIFF_V), BF16),
        scratch_shapes=_flash_scratch(DIFF_V, 2 * DIFF_BQ),
        compiler_params=_params(("parallel", "parallel", "parallel")),
        name="diff_flash",
    )(q1, q2, k, vt, lam_q, lam_k, g_sub)


def _merge_kernel(om_ref, od_ref, gm_ref, gd_ref, wm_ref, wd_ref, o_ref):
    a = jnp.dot(om_ref[...], wm_ref[...], preferred_element_type=F32)
    b = jnp.dot(od_ref[...], wd_ref[...], preferred_element_type=F32)
    o_ref[...] = (gm_ref[...] * a + gd_ref[...] * b).astype(BF16)


def _outproj_ln_kernel(x_ref, m_ref, w_ref, g_ref, b_ref, o_ref):
    h = jnp.dot(m_ref[...], w_ref[...], preferred_element_type=F32)
    o_ref[...] = _layernorm_rows(DN_ALPHA * x_ref[...] + h, g_ref[...], b_ref[...])


def _ffn_ln_kernel(x_ref, w1_ref, w2_ref, g_ref, b_ref, o_ref, xb_ref, acc_ref):
    j = pl.program_id(1)

    @pl.when(j == 0)
    def _():
        xb_ref[...] = x_ref[...].astype(BF16)
        acc_ref[...] = jnp.zeros_like(acc_ref)

    h = jnp.maximum(jnp.dot(xb_ref[...], w1_ref[...], preferred_element_type=F32), 0.0)
    acc_ref[...] += jnp.dot((h * h).astype(BF16), w2_ref[...], preferred_element_type=F32)

    @pl.when(j == pl.num_programs(1) - 1)
    def _():
        o_ref[...] = _layernorm_rows(DN_ALPHA * x_ref[...] + acc_ref[...], g_ref[...], b_ref[...])


def _merge(o_mla, o_diff, gates, w_br_mla, w_br_diff):
    t, tm = o_mla.shape[0], 256
    return pl.pallas_call(
        _merge_kernel,
        grid=(t // tm,),
        in_specs=[_row_spec(tm, o_mla.shape[1]), _row_spec(tm, o_diff.shape[1]),
                  pl.BlockSpec((tm, D_MODEL), lambda i: (i, 0)),
                  pl.BlockSpec((tm, D_MODEL), lambda i: (i, 1)),
                  _full_spec(w_br_mla.shape), _full_spec(w_br_diff.shape)],
        out_specs=_row_spec(tm, D_MODEL),
        out_shape=jax.ShapeDtypeStruct((t, D_MODEL), BF16),
        compiler_params=_params(("parallel",)),
        name="branch_merge",
    )(o_mla, o_diff, gates, gates, w_br_mla, w_br_diff)


def _outproj_ln(x, merged, w_out, g, b):
    t, tm = x.shape[0], 256
    return pl.pallas_call(
        _outproj_ln_kernel,
        grid=(t // tm,),
        in_specs=[_row_spec(tm, D_MODEL), _row_spec(tm, D_MODEL), _full_spec(w_out.shape),
                  _full_spec(g.shape), _full_spec(b.shape)],
        out_specs=_row_spec(tm, D_MODEL),
        out_shape=jax.ShapeDtypeStruct((t, D_MODEL), F32),
        compiler_params=_params(("parallel",)),
        name="outproj_ln",
    )(x, merged, w_out, g, b)


def _ffn_ln(x, w1, w2, g, b):
    t, tm, tf = x.shape[0], 512, 1024
    return pl.pallas_call(
        _ffn_ln_kernel,
        grid=(t // tm, D_FF // tf),
        in_specs=[pl.BlockSpec((tm, D_MODEL), lambda i, j: (i, 0)),
                  pl.BlockSpec((D_MODEL, tf), lambda i, j: (0, j)),
                  pl.BlockSpec((tf, D_MODEL), lambda i, j: (j, 0)),
                  pl.BlockSpec((1, D_MODEL), lambda i, j: (0, 0)),
                  pl.BlockSpec((1, D_MODEL), lambda i, j: (0, 0))],
        out_specs=pl.BlockSpec((tm, D_MODEL), lambda i, j: (i, 0)),
        out_shape=jax.ShapeDtypeStruct((t, D_MODEL), F32),
        scratch_shapes=[pltpu.VMEM((tm, D_MODEL), BF16), pltpu.VMEM((tm, D_MODEL), F32)],
        compiler_params=_params(("parallel", "arbitrary")),
        name="ffn_ln",
    )(x, w1, w2, g, b)


def _rope_tables(seq, rot_dim, group):
    half = rot_dim // 2
    inv_freq = ROPE_THETA ** (-jnp.arange(0, rot_dim, 2, dtype=F32) / rot_dim)
    ang = jnp.arange(seq, dtype=F32)[:, None] * inv_freq[None, :]
    cos, sin = jnp.cos(ang), jnp.sin(ang)
    zeros = lambda n: jnp.zeros((seq, n), F32)
    c = jnp.concatenate([cos, cos, jnp.ones((seq, group - rot_dim), F32)], axis=1)
    s_fwd = jnp.concatenate([-sin, zeros(group - half)], axis=1)
    s_bwd = jnp.concatenate([zeros(half), sin, zeros(group - rot_dim)], axis=1)
    return jnp.stack([c, s_fwd, s_bwd])


def _prep_weights(w_in, b_gate, g_qa, w_qb, g_kva, w_kvb, g_sub, w_br_mla, w_br_diff, w_out,
                  ln1_g, ln1_b, w_ff1, w_ff2, ln2_g, ln2_b, l):
    c0 = C_QA
    c1 = c0 + C_KVA
    c2 = c1 + C_DQ
    c3 = c2 + C_DK
    c4 = c3 + C_DV
    wi = w_in[l]
    qb = w_qb[l].reshape(MLA_Q_LORA, MLA_HEADS, MLA_NOPE + MLA_ROPE)
    qb_nope = qb[:, :, :MLA_NOPE].reshape(MLA_Q_LORA, MLA_HEADS * MLA_NOPE)
    qb_rope = jnp.pad(qb[:, :, MLA_NOPE:], ((0, 0), (0, 0), (0, LANES - MLA_ROPE)))
    qb_rope = qb_rope.reshape(MLA_Q_LORA, MLA_HEADS * LANES)
    kvb = w_kvb[l].reshape(MLA_KV_LORA, MLA_HEADS, MLA_NOPE + MLA_V)
    kvb_k = kvb[:, :, :MLA_NOPE].reshape(MLA_KV_LORA, MLA_HEADS * MLA_NOPE)
    kvb_v = kvb[:, :, MLA_NOPE:].reshape(MLA_KV_LORA, MLA_HEADS * MLA_V)
    row = lambda v: v[l].reshape(1, -1)
    return dict(
        wqa=wi[:, :c0].astype(BF16),
        wkva=jnp.pad(wi[:, c0:c1], ((0, 0), (0, LANES - MLA_ROPE))).astype(BF16),
        wdq=wi[:, c1:c2].astype(BF16),
        wdk=wi[:, c2:c3].astype(BF16),
        wdv=wi[:, c3:c4].astype(BF16),
        wgate=wi[:, c4:].astype(BF16),
        b_gate=row(b_gate),
        g_qa=row(g_qa),
        wqb=jnp.concatenate([qb_nope, qb_rope], axis=1).astype(BF16),
        g_kva=row(g_kva),
        wkvb=jnp.concatenate([kvb_k, kvb_v], axis=1).astype(BF16),
        g_sub=row(g_sub),
        w_br_mla=w_br_mla[l].astype(BF16),
        w_br_diff=w_br_diff[l].astype(BF16),
        w_out=w_out[l].astype(BF16),
        ln1_g=row(ln1_g), ln1_b=row(ln1_b),
        w_ff1=w_ff1[l].astype(BF16), w_ff2=w_ff2[l].astype(BF16),
        ln2_g=row(ln2_g), ln2_b=row(ln2_b),
    )


def _layer(x3, w, lam_q, lam_k, lam_init):
    batch, seq, _ = x3.shape
    x = x3.reshape(batch * seq, D_MODEL)
    tab_mla = jnp.pad(_rope_tables(seq, MLA_ROPE, MLA_ROPE), ((0, 0), (0, 0), (0, LANES - MLA_ROPE)))
    tab_diff = jnp.tile(_rope_tables(seq, DIFF_ROT, DIFF_QK), (1, 1, LANES // DIFF_QK))

    q = _q_proj(x, w["wqa"], w["g_qa"], w["wqb"], tab_mla, seq)
    k, vt = _kv_proj(x, w["wkva"], w["g_kva"], w["wkvb"], tab_mla, seq)
    o_mla = _mla_flash(q, k, vt, batch, seq)

    q1, q2 = _dq_proj(x, w["wdq"], tab_diff, seq)
    dk = _dk_proj(x, w["wdk"], tab_diff, seq)
    dvt = _dv_proj(x, w["wdv"])
    o_diff = _diff_flash(q1, q2, dk, dvt, lam_q, lam_k, w["g_sub"], batch, seq, lam_init)

    gates = _gates(x, w["wgate"], w["b_gate"])
    merged = _merge(o_mla, o_diff, gates, w["w_br_mla"], w["w_br_diff"])
    x1 = _outproj_ln(x, merged, w["w_out"], w["ln1_g"], w["ln1_b"])
    y = _ffn_ln(x1, w["w_ff1"], w["w_ff2"], w["ln2_g"], w["ln2_b"])
    return y.reshape(batch, seq, D_MODEL)


def kernel(x_prompt, x_sample, w_in, b_gate, g_qa, w_qb, g_kva, w_kvb, lam_q, lam_k, g_sub,
           w_br_mla, w_br_diff, w_out, ln1_g, ln1_b, w_ff1, w_ff2, ln2_g, ln2_b):
    outs = [x_prompt, x_sample]
    for l in range(DEPTH):
        lam_init = 0.8 - 0.6 * math.exp(-0.3 * l)
        w = _prep_weights(w_in, b_gate, g_qa, w_qb, g_kva, w_kvb, g_sub, w_br_mla, w_br_diff, w_out,
                          ln1_g, ln1_b, w_ff1, w_ff2, ln2_g, ln2_b, l)
        outs = [_layer(x3, w, lam_q[l], lam_k[l], lam_init) for x3 in outs]
    return tuple(outs)
```

```python
import functools
import math

import jax
import jax.numpy as jnp
from jax import lax
from jax.experimental import pallas as pl
from jax.experimental.pallas import tpu as pltpu

D_MODEL = 2048
DEPTH = 1
MLA_HEADS = 8
MLA_Q_LORA = 768
MLA_KV_LORA = 512
MLA_NOPE = 128
MLA_ROPE = 64
MLA_V = 128
DIFF_HEADS = 8
DIFF_QK = 64
DIFF_V = 2 * DIFF_QK
DIFF_ROT = DIFF_QK // 4
D_FF = 4 * D_MODEL
ROPE_THETA = 500000.0
LN_EPS = 1e-5
RMS_EPS = 1e-6
DN_ALPHA = (2.0 * DEPTH) ** 0.25
LOG2_E = math.log2(math.e)

C_QA = MLA_Q_LORA
C_KVA = MLA_KV_LORA + MLA_ROPE
C_DQ = DIFF_HEADS * 2 * DIFF_QK
C_DK = DIFF_HEADS * 2 * DIFF_QK
C_DV = DIFF_HEADS * DIFF_V

LANES = 128
HEAD_PAD = 2 * LANES
KEY_CHUNK = 1024
KEY_TILE = 256
SUBLANES = 8
VT_TM = 512
MLA_BQ = 512
DIFF_BQ = 256
VMEM_LIMIT = 56 * 1024 * 1024

F32 = jnp.float32
BF16 = jnp.bfloat16


def _params(sem):
    return pltpu.CompilerParams(dimension_semantics=sem, vmem_limit_bytes=VMEM_LIMIT)


def _rope_mix(x, tab_ref, shift):
    n = x.shape[-1]
    fwd = pltpu.roll(x, n - shift, 1)
    bwd = pltpu.roll(x, shift, 1)
    return x * tab_ref[0] + fwd * tab_ref[1] + bwd * tab_ref[2]


def _rmsnorm_rows(x, g):
    ms = jnp.mean(x * x, axis=-1, keepdims=True)
    return x * lax.rsqrt(ms + RMS_EPS) * g


def _layernorm_rows(z, g, b):
    mu = jnp.mean(z, axis=-1, keepdims=True)
    zc = z - mu
    var = jnp.mean(zc * zc, axis=-1, keepdims=True)
    return zc * lax.rsqrt(var + LN_EPS) * g + b


def _q_proj_kernel(x_ref, wqa_ref, gqa_ref, wqb_ref, tab_ref, qt_ref, *, scale):
    xb = x_ref[...].astype(BF16)
    qa = jnp.dot(xb, wqa_ref[...], preferred_element_type=F32)
    qn = _rmsnorm_rows(qa, gqa_ref[...]).astype(BF16)
    q = jnp.dot(qn, wqb_ref[...], preferred_element_type=F32)
    nope_w = MLA_HEADS * MLA_NOPE
    for h in range(MLA_HEADS):
        nope = q[:, h * MLA_NOPE:(h + 1) * MLA_NOPE]
        rope = _rope_mix(q[:, nope_w + h * LANES: nope_w + (h + 1) * LANES], tab_ref, MLA_ROPE // 2)
        qt_ref[h, 0, :LANES, :] = (nope * scale).T.astype(BF16)
        qt_ref[h, 0, LANES:, :] = (rope * scale).T.astype(BF16)


def _kv_proj_kernel(x_ref, wkva_ref, gkva_ref, wkvb_ref, tab_ref, k_ref, vt_ref):
    xb = x_ref[...].astype(BF16)
    kva = jnp.dot(xb, wkva_ref[...], preferred_element_type=F32)
    ckv = _rmsnorm_rows(kva[:, :MLA_KV_LORA], gkva_ref[...]).astype(BF16)
    krope = _rope_mix(kva[:, MLA_KV_LORA:], tab_ref, MLA_ROPE // 2).astype(BF16)
    kv = jnp.dot(ckv, wkvb_ref[...], preferred_element_type=F32)
    nope_w = MLA_HEADS * MLA_NOPE
    for h in range(MLA_HEADS):
        k_ref[:, h * HEAD_PAD: h * HEAD_PAD + LANES] = kv[:, h * MLA_NOPE:(h + 1) * MLA_NOPE].astype(BF16)
        k_ref[:, h * HEAD_PAD + LANES:(h + 1) * HEAD_PAD] = krope
    vt = kv[:, nope_w:].T
    vt_ref[:, 0] = vt.reshape(MLA_HEADS, MLA_V, vt.shape[-1]).astype(BF16)


def _dq_proj_kernel(x_ref, w_ref, tab_ref, qt_ref, *, scale):
    xb = x_ref[...].astype(BF16)
    dq = jnp.dot(xb, w_ref[...], preferred_element_type=F32)
    first_map = lax.broadcasted_iota(jnp.int32, (dq.shape[0], LANES), 1) < DIFF_QK
    for h in range(DIFF_HEADS):
        r = _rope_mix(dq[:, h * LANES:(h + 1) * LANES], tab_ref, DIFF_ROT // 2) * scale
        r1 = jnp.where(first_map, r, 0.0).T.astype(BF16)
        r2 = jnp.where(first_map, 0.0, r).T.astype(BF16)
        for blk in range(qt_ref.shape[1]):
            cols = slice(blk * DIFF_BQ, (blk + 1) * DIFF_BQ)
            qt_ref[h, blk, :, :DIFF_BQ] = r1[:, cols]
            qt_ref[h, blk, :, DIFF_BQ:] = r2[:, cols]


def _dk_proj_kernel(x_ref, w_ref, tab_ref, k_ref):
    xb = x_ref[...].astype(BF16)
    dk = jnp.dot(xb, w_ref[...], preferred_element_type=F32)
    for h in range(DIFF_HEADS):
        k_ref[:, h * LANES:(h + 1) * LANES] = _rope_mix(
            dk[:, h * LANES:(h + 1) * LANES], tab_ref, DIFF_ROT // 2).astype(BF16)


def _dv_proj_kernel(x_ref, w_ref, vt_ref):
    xb = x_ref[...].astype(BF16)
    dv = jnp.dot(xb, w_ref[...], preferred_element_type=F32)
    vt = dv.T
    vt_ref[:, 0] = vt.reshape(DIFF_HEADS, DIFF_V, vt.shape[-1]).astype(BF16)


def _gate_kernel(x_ref, w_ref, b_ref, g_ref):
    xb = x_ref[...].astype(BF16)
    z = jnp.dot(xb, w_ref[...], preferred_element_type=F32) + b_ref[...]
    g_ref[...] = (1.0 / (1.0 + jnp.exp(-z))).astype(g_ref.dtype)


def _row_spec(tm, width):
    return pl.BlockSpec((tm, width), lambda i: (i, 0))


def _full_spec(shape):
    nd = len(shape)
    return pl.BlockSpec(shape, lambda i: (0,) * nd)


def _tab_spec(tm, s_tiles):
    return pl.BlockSpec((3, tm, LANES), lambda i: (0, i % s_tiles, 0))


def _vt_out(t, tm):
    per_chunk = KEY_CHUNK // tm
    shape = jax.ShapeDtypeStruct((MLA_HEADS, t // KEY_CHUNK, MLA_V, KEY_CHUNK), BF16)
    spec = pl.BlockSpec((MLA_HEADS, 1, MLA_V, tm), lambda i: (0, i // per_chunk, 0, i % per_chunk))
    return shape, spec


def _q_proj(x, wqa, gqa, wqb, tab, seq):
    t, tm = x.shape[0], MLA_BQ
    return pl.pallas_call(
        functools.partial(_q_proj_kernel, scale=LOG2_E * (MLA_NOPE + MLA_ROPE) ** -0.5),
        grid=(t // tm,),
        in_specs=[_row_spec(tm, D_MODEL), _full_spec(wqa.shape), _full_spec(gqa.shape),
                  _full_spec(wqb.shape), _tab_spec(tm, seq // tm)],
        out_specs=pl.BlockSpec((MLA_HEADS, 1, HEAD_PAD, tm), lambda i: (0, i, 0, 0)),
        out_shape=jax.ShapeDtypeStruct((MLA_HEADS, t // tm, HEAD_PAD, tm), BF16),
        compiler_params=_params(("parallel",)),
        name="mla_q_proj",
    )(x, wqa, gqa, wqb, tab)


def _kv_proj(x, wkva, gkva, wkvb, tab, seq):
    t, tm = x.shape[0], VT_TM
    vt_shape, vt_spec = _vt_out(t, tm)
    return pl.pallas_call(
        _kv_proj_kernel,
        grid=(t // tm,),
        in_specs=[_row_spec(tm, D_MODEL), _full_spec(wkva.shape), _full_spec(gkva.shape),
                  _full_spec(wkvb.shape), _tab_spec(tm, seq // tm)],
        out_specs=[_row_spec(tm, MLA_HEADS * HEAD_PAD), vt_spec],
        out_shape=[jax.ShapeDtypeStruct((t, MLA_HEADS * HEAD_PAD), BF16), vt_shape],
        compiler_params=_params(("parallel",)),
        name="mla_kv_proj",
    )(x, wkva, gkva, wkvb, tab)


def _dq_proj(x, w, tab, seq):
    t, tm = x.shape[0], 512
    blocks = tm // DIFF_BQ
    return pl.pallas_call(
        functools.partial(_dq_proj_kernel, scale=LOG2_E * DIFF_QK ** -0.5),
        grid=(t // tm,),
        in_specs=[_row_spec(tm, D_MODEL), _full_spec(w.shape), _tab_spec(tm, seq // tm)],
        out_specs=pl.BlockSpec((DIFF_HEADS, blocks, LANES, 2 * DIFF_BQ), lambda i: (0, i, 0, 0)),
        out_shape=jax.ShapeDtypeStruct((DIFF_HEADS, t // DIFF_BQ, LANES, 2 * DIFF_BQ), BF16),
        compiler_params=_params(("parallel",)),
        name="diff_q_proj",
    )(x, w, tab)


def _dk_proj(x, w, tab, seq):
    t, tm = x.shape[0], 512
    return pl.pallas_call(
        _dk_proj_kernel,
        grid=(t // tm,),
        in_specs=[_row_spec(tm, D_MODEL), _full_spec(w.shape), _tab_spec(tm, seq // tm)],
        out_specs=_row_spec(tm, C_DK),
        out_shape=jax.ShapeDtypeStruct((t, C_DK), BF16),
        compiler_params=_params(("parallel",)),
        name="diff_k_proj",
    )(x, w, tab)


def _dv_proj(x, w):
    t, tm = x.shape[0], VT_TM
    vt_shape, vt_spec = _vt_out(t, tm)
    return pl.pallas_call(
        _dv_proj_kernel,
        grid=(t // tm,),
        in_specs=[_row_spec(tm, D_MODEL), _full_spec(w.shape)],
        out_specs=vt_spec,
        out_shape=vt_shape,
        compiler_params=_params(("parallel",)),
        name="diff_v_proj",
    )(x, w)


def _gates(x, w, b):
    t, tm = x.shape[0], 512
    n = w.shape[1]
    tn = D_MODEL
    return pl.pallas_call(
        _gate_kernel,
        grid=(n // tn, t // tm),
        in_specs=[pl.BlockSpec((tm, D_MODEL), lambda j, i: (i, 0)),
                  pl.BlockSpec((D_MODEL, tn), lambda j, i: (0, j)),
                  pl.BlockSpec((1, tn), lambda j, i: (0, j))],
        out_specs=pl.BlockSpec((tm, tn), lambda j, i: (i, j)),
        out_shape=jax.ShapeDtypeStruct((t, n), BF16),
        compiler_params=_params(("parallel", "parallel")),
        name="gates",
    )(x, w, b)


def _flash_head(qt_ref, k_ref, vt_ref, s_ref, p_ref, m_ref, l_ref, acc_ref):
    nq, _, nc = qt_ref.shape
    n_chunks = vt_ref.shape[0]
    n_steps = n_chunks * nq
    n_tiles = KEY_CHUNK // KEY_TILE
    block_bits = nq.bit_length() - 1
    assert nq == 1 << block_bits and n_steps % 2 == 0

    def chunk_and_block(g):
        return lax.shift_right_logical(g, block_bits), lax.bitwise_and(g, nq - 1)

    def sublane_groups(x):
        return x.reshape(KEY_TILE // SUBLANES, SUBLANES, nc)

    def score_tile(g, t):
        chunk, blk = chunk_and_block(g)
        start = pl.multiple_of(chunk * KEY_CHUNK + t * KEY_TILE, KEY_TILE)
        return jnp.dot(k_ref[pl.ds(start, KEY_TILE), :], qt_ref[blk], preferred_element_type=F32)

    def add_pv(g_prev, alpha_prev):
        chunk, blk = chunk_and_block(g_prev)
        pv = jnp.dot(vt_ref[chunk], p_ref[...], preferred_element_type=F32)
        acc_ref[blk] = alpha_prev * acc_ref[blk] + pv

    def step(g, carry):
        alpha_prev, s_max = carry
        _, blk = chunk_and_block(g)
        m = m_ref[blk]
        m_new = jnp.maximum(m, s_max)
        alpha = jnp.exp2(m - m_new)
        m_ref[blk] = m_new
        add_pv(jnp.maximum(g - 1, 0), alpha_prev)
        g_next = jnp.minimum(g + 1, n_steps - 1)
        next_max = None
        p_sum = None
        for t in range(n_tiles):
            rows = pl.ds(t * KEY_TILE, KEY_TILE)
            p = jnp.exp2(s_ref[rows, :] - m_new)
            p_ref[rows, :] = p.astype(BF16)
            tile_sum = jnp.sum(sublane_groups(p), axis=0)
            p_sum = tile_sum if t == 0 else p_sum + tile_sum
            s_tile = score_tile(g_next, t)
            s_ref[rows, :] = s_tile
            tile_max = jnp.max(sublane_groups(s_tile), axis=0)
            next_max = tile_max if t == 0 else jnp.maximum(next_max, tile_max)
        l_ref[blk] = alpha * l_ref[blk] + jnp.sum(p_sum, axis=0, keepdims=True)
        return alpha, jnp.max(next_max, axis=0, keepdims=True)

    first_max = None
    for t in range(n_tiles):
        s_tile = score_tile(0, t)
        s_ref[pl.ds(t * KEY_TILE, KEY_TILE), :] = s_tile
        tile_max = jnp.max(sublane_groups(s_tile), axis=0)
        first_max = tile_max if t == 0 else jnp.maximum(first_max, tile_max)
    p_ref[...] = jnp.zeros(p_ref.shape, BF16)
    m_ref[...] = jnp.full(m_ref.shape, -jnp.inf, F32)
    l_ref[...] = jnp.zeros(l_ref.shape, F32)
    acc_ref[...] = jnp.zeros(acc_ref.shape, F32)

    init = (jnp.ones((1, nc), F32), jnp.max(first_max, axis=0, keepdims=True))
    alpha_last, _ = lax.fori_loop(0, n_steps, step, init, unroll=2)
    add_pv(n_steps - 1, alpha_last)


def _mla_flash_kernel(qt_ref, k_ref, vt_ref, o_ref, s_ref, p_ref, m_ref, l_ref, acc_ref):
    _flash_head(qt_ref, k_ref, vt_ref, s_ref, p_ref, m_ref, l_ref, acc_ref)
    nq, _, nc = qt_ref.shape

    def finish(blk, _):
        o = acc_ref[blk] * (1.0 / l_ref[blk])
        o_ref[pl.ds(pl.multiple_of(blk * nc, nc), nc), :] = o.T.astype(BF16)
        return 0

    lax.fori_loop(0, nq, finish, 0)


def _diff_flash_kernel(qt_ref, k_ref, vt_ref, lq_ref, lk_ref, gsub_ref, o_ref,
                       s_ref, p_ref, m_ref, l_ref, acc_ref, *, lam_init):
    _flash_head(qt_ref, k_ref, vt_ref, s_ref, p_ref, m_ref, l_ref, acc_ref)
    nq, _, nc = qt_ref.shape
    bq = nc // 2
    lam_dot = jnp.sum(lq_ref[...] * lk_ref[...], axis=-1, keepdims=True)
    lam_exp = jnp.exp(lam_dot)
    lam = lam_exp[0:1, :] - lam_exp[1:2, :] + lam_init

    def finish(blk, _):
        on = acc_ref[blk] * (1.0 / l_ref[blk])
        a = on[:, :bq] - lam * on[:, bq:]
        ms = jnp.mean(a * a, axis=0, keepdims=True)
        y = (a * lax.rsqrt(ms + RMS_EPS)).T * gsub_ref[...]
        o_ref[pl.ds(pl.multiple_of(blk * bq, bq), bq), :] = (y * (1.0 - lam_init)).astype(BF16)
        return 0

    lax.fori_loop(0, nq, finish, 0)


def _flash_call(kernel, qt, k, vt, extra, batch, seq, dk, dv, nc, name):
    heads = qt.shape[0]
    nq = qt.shape[1] // batch
    n_chunks = seq // KEY_CHUNK
    once = pl.Buffered(1)
    small = lambda shape: pl.BlockSpec(shape, lambda b, h: (0, 0))
    return pl.pallas_call(
        kernel,
        grid=(batch, heads),
        in_specs=[pl.BlockSpec((None, nq, dk, nc), lambda b, h: (h, b, 0, 0), pipeline_mode=once),
                  pl.BlockSpec((seq, dk), lambda b, h: (b, h), pipeline_mode=once),
                  pl.BlockSpec((None, n_chunks, dv, KEY_CHUNK), lambda b, h: (h, b, 0, 0),
                               pipeline_mode=once)] + [small(e.shape) for e in extra],
        out_specs=pl.BlockSpec((seq, dv), lambda b, h: (b, h)),
        out_shape=jax.ShapeDtypeStruct((batch * seq, heads * dv), BF16),
        scratch_shapes=[pltpu.VMEM((KEY_CHUNK, nc), F32), pltpu.VMEM((KEY_CHUNK, nc), BF16),
                        pltpu.VMEM((nq, 1, nc), F32), pltpu.VMEM((nq, 1, nc), F32),
                        pltpu.VMEM((nq, dv, nc), F32)],
        compiler_params=_params(("parallel", "parallel")),
        name=name,
    )(qt, k, vt, *extra)


def _mla_flash(qt, k, vt, batch, seq):
    return _flash_call(_mla_flash_kernel, qt, k, vt, (), batch, seq, HEAD_PAD, MLA_V, MLA_BQ, "mla_flash")


def _diff_flash(qt, k, vt, lam_q, lam_k, g_sub, batch, seq, lam_init):
    return _flash_call(functools.partial(_diff_flash_kernel, lam_init=lam_init), qt, k, vt,
                       (lam_q, lam_k, g_sub), batch, seq, LANES, DIFF_V, 2 * DIFF_BQ, "diff_flash")


def _merge_kernel(om_ref, od_ref, gm_ref, gd_ref, wm_ref, wd_ref, o_ref):
    a = jnp.dot(om_ref[...], wm_ref[...], preferred_element_type=F32)
    b = jnp.dot(od_ref[...], wd_ref[...], preferred_element_type=F32)
    o_ref[...] = (gm_ref[...] * a + gd_ref[...] * b).astype(BF16)


def _outproj_ln_kernel(x_ref, m_ref, w_ref, g_ref, b_ref, o_ref):
    h = jnp.dot(m_ref[...], w_ref[...], preferred_element_type=F32)
    o_ref[...] = _layernorm_rows(DN_ALPHA * x_ref[...] + h, g_ref[...], b_ref[...])


def _ffn_ln_kernel(x_ref, w1_ref, w2_ref, g_ref, b_ref, o_ref, xb_ref, acc_ref):
    j = pl.program_id(1)

    @pl.when(j == 0)
    def _():
        xb_ref[...] = x_ref[...].astype(BF16)
        acc_ref[...] = jnp.zeros_like(acc_ref)

    h = jnp.maximum(jnp.dot(xb_ref[...], w1_ref[...], preferred_element_type=F32), 0.0)
    acc_ref[...] += jnp.dot((h * h).astype(BF16), w2_ref[...], preferred_element_type=F32)

    @pl.when(j == pl.num_programs(1) - 1)
    def _():
        o_ref[...] = _layernorm_rows(DN_ALPHA * x_ref[...] + acc_ref[...], g_ref[...], b_ref[...])


def _merge(o_mla, o_diff, gates, w_br_mla, w_br_diff):
    t, tm = o_mla.shape[0], 256
    return pl.pallas_call(
        _merge_kernel,
        grid=(t // tm,),
        in_specs=[_row_spec(tm, o_mla.shape[1]), _row_spec(tm, o_diff.shape[1]),
                  pl.BlockSpec((tm, D_MODEL), lambda i: (i, 0)),
                  pl.BlockSpec((tm, D_MODEL), lambda i: (i, 1)),
                  _full_spec(w_br_mla.shape), _full_spec(w_br_diff.shape)],
        out_specs=_row_spec(tm, D_MODEL),
        out_shape=jax.ShapeDtypeStruct((t, D_MODEL), BF16),
        compiler_params=_params(("parallel",)),
        name="branch_merge",
    )(o_mla, o_diff, gates, gates, w_br_mla, w_br_diff)


def _outproj_ln(x, merged, w_out, g, b):
    t, tm = x.shape[0], 256
    return pl.pallas_call(
        _outproj_ln_kernel,
        grid=(t // tm,),
        in_specs=[_row_spec(tm, D_MODEL), _row_spec(tm, D_MODEL), _full_spec(w_out.shape),
                  _full_spec(g.shape), _full_spec(b.shape)],
        out_specs=_row_spec(tm, D_MODEL),
        out_shape=jax.ShapeDtypeStruct((t, D_MODEL), F32),
        compiler_params=_params(("parallel",)),
        name="outproj_ln",
    )(x, merged, w_out, g, b)


def _ffn_ln(x, w1, w2, g, b):
    t, tm, tf = x.shape[0], 512, 1024
    return pl.pallas_call(
        _ffn_ln_kernel,
        grid=(t // tm, D_FF // tf),
        in_specs=[pl.BlockSpec((tm, D_MODEL), lambda i, j: (i, 0)),
                  pl.BlockSpec((D_MODEL, tf), lambda i, j: (0, j)),
                  pl.BlockSpec((tf, D_MODEL), lambda i, j: (j, 0)),
                  pl.BlockSpec((1, D_MODEL), lambda i, j: (0, 0)),
                  pl.BlockSpec((1, D_MODEL), lambda i, j: (0, 0))],
        out_specs=pl.BlockSpec((tm, D_MODEL), lambda i, j: (i, 0)),
        out_shape=jax.ShapeDtypeStruct((t, D_MODEL), F32),
        scratch_shapes=[pltpu.VMEM((tm, D_MODEL), BF16), pltpu.VMEM((tm, D_MODEL), F32)],
        compiler_params=_params(("parallel", "arbitrary")),
        name="ffn_ln",
    )(x, w1, w2, g, b)


def _rope_tables(seq, rot_dim, group):
    half = rot_dim // 2
    inv_freq = ROPE_THETA ** (-jnp.arange(0, rot_dim, 2, dtype=F32) / rot_dim)
    ang = jnp.arange(seq, dtype=F32)[:, None] * inv_freq[None, :]
    cos, sin = jnp.cos(ang), jnp.sin(ang)
    zeros = lambda n: jnp.zeros((seq, n), F32)
    c = jnp.concatenate([cos, cos, jnp.ones((seq, group - rot_dim), F32)], axis=1)
    s_fwd = jnp.concatenate([-sin, zeros(group - half)], axis=1)
    s_bwd = jnp.concatenate([zeros(half), sin, zeros(group - rot_dim)], axis=1)
    return jnp.stack([c, s_fwd, s_bwd])


def _prep_weights(w_in, b_gate, g_qa, w_qb, g_kva, w_kvb, g_sub, w_br_mla, w_br_diff, w_out,
                  ln1_g, ln1_b, w_ff1, w_ff2, ln2_g, ln2_b, l):
    c0 = C_QA
    c1 = c0 + C_KVA
    c2 = c1 + C_DQ
    c3 = c2 + C_DK
    c4 = c3 + C_DV
    wi = w_in[l]
    qb = w_qb[l].reshape(MLA_Q_LORA, MLA_HEADS, MLA_NOPE + MLA_ROPE)
    qb_nope = qb[:, :, :MLA_NOPE].reshape(MLA_Q_LORA, MLA_HEADS * MLA_NOPE)
    qb_rope = jnp.pad(qb[:, :, MLA_NOPE:], ((0, 0), (0, 0), (0, LANES - MLA_ROPE)))
    qb_rope = qb_rope.reshape(MLA_Q_LORA, MLA_HEADS * LANES)
    kvb = w_kvb[l].reshape(MLA_KV_LORA, MLA_HEADS, MLA_NOPE + MLA_V)
    kvb_k = kvb[:, :, :MLA_NOPE].reshape(MLA_KV_LORA, MLA_HEADS * MLA_NOPE)
    kvb_v = kvb[:, :, MLA_NOPE:].reshape(MLA_KV_LORA, MLA_HEADS * MLA_V)
    row = lambda v: v[l].reshape(1, -1)
    return dict(
        wqa=wi[:, :c0].astype(BF16),
        wkva=jnp.pad(wi[:, c0:c1], ((0, 0), (0, LANES - MLA_ROPE))).astype(BF16),
        wdq=wi[:, c1:c2].astype(BF16),
        wdk=wi[:, c2:c3].astype(BF16),
        wdv=wi[:, c3:c4].astype(BF16),
        wgate=wi[:, c4:].astype(BF16),
        b_gate=row(b_gate),
        g_qa=row(g_qa),
        wqb=jnp.concatenate([qb_nope, qb_rope], axis=1).astype(BF16),
        g_kva=row(g_kva),
        wkvb=jnp.concatenate([kvb_k, kvb_v], axis=1).astype(BF16),
        g_sub=row(g_sub),
        w_br_mla=w_br_mla[l].astype(BF16),
        w_br_diff=w_br_diff[l].astype(BF16),
        w_out=w_out[l].astype(BF16),
        ln1_g=row(ln1_g), ln1_b=row(ln1_b),
        w_ff1=w_ff1[l].astype(BF16), w_ff2=w_ff2[l].astype(BF16),
        ln2_g=row(ln2_g), ln2_b=row(ln2_b),
    )


def _layer(x3, w, lam_q, lam_k, lam_init):
    batch, seq, _ = x3.shape
    x = x3.reshape(batch * seq, D_MODEL)
    tab_mla = jnp.pad(_rope_tables(seq, MLA_ROPE, MLA_ROPE), ((0, 0), (0, 0), (0, LANES - MLA_ROPE)))
    tab_diff = jnp.tile(_rope_tables(seq, DIFF_ROT, DIFF_QK), (1, 1, LANES // DIFF_QK))

    qt = _q_proj(x, w["wqa"], w["g_qa"], w["wqb"], tab_mla, seq)
    k, vt = _kv_proj(x, w["wkva"], w["g_kva"], w["wkvb"], tab_mla, seq)
    o_mla = _mla_flash(qt, k, vt, batch, seq)

    dqt = _dq_proj(x, w["wdq"], tab_diff, seq)
    dk = _dk_proj(x, w["wdk"], tab_diff, seq)
    dvt = _dv_proj(x, w["wdv"])
    o_diff = _diff_flash(dqt, dk, dvt, lam_q, lam_k, w["g_sub"], batch, seq, lam_init)

    gates = _gates(x, w["wgate"], w["b_gate"])
    merged = _merge(o_mla, o_diff, gates, w["w_br_mla"], w["w_br_diff"])
    x1 = _outproj_ln(x, merged, w["w_out"], w["ln1_g"], w["ln1_b"])
    y = _ffn_ln(x1, w["w_ff1"], w["w_ff2"], w["ln2_g"], w["ln2_b"])
    return y.reshape(batch, seq, D_MODEL)


def kernel(x_prompt, x_sample, w_in, b_gate, g_qa, w_qb, g_kva, w_kvb, lam_q, lam_k, g_sub,
           w_br_mla, w_br_diff, w_out, ln1_g, ln1_b, w_ff1, w_ff2, ln2_g, ln2_b):
    outs = [x_prompt, x_sample]
    for l in range(DEPTH):
        lam_init = 0.8 - 0.6 * math.exp(-0.3 * l)
        w = _prep_weights(w_in, b_gate, g_qa, w_qb, g_kva, w_kvb, g_sub, w_br_mla, w_br_diff, w_out,
                          ln1_g, ln1_b, w_ff1, w_ff2, ln2_g, ln2_b, l)
        outs = [_layer(x3, w, lam_q[l], lam_k[l], lam_init) for x3 in outs]
    return tuple(outs)
```

```python
import functools
import math

import jax
import jax.numpy as jnp
from jax import lax
from jax.experimental import pallas as pl
from jax.experimental.pallas import tpu as pltpu

D_MODEL = 2048
DEPTH = 1
MLA_HEADS = 8
MLA_Q_LORA = 768
MLA_KV_LORA = 512
MLA_NOPE = 128
MLA_ROPE = 64
MLA_V = 128
DIFF_HEADS = 8
DIFF_QK = 64
DIFF_V = 2 * DIFF_QK
DIFF_ROT = DIFF_QK // 4
D_FF = 4 * D_MODEL
ROPE_THETA = 500000.0
LN_EPS = 1e-5
RMS_EPS = 1e-6
DN_ALPHA = (2.0 * DEPTH) ** 0.25
LOG2_E = math.log2(math.e)

C_QA = MLA_Q_LORA
C_KVA = MLA_KV_LORA + MLA_ROPE
C_DQ = DIFF_HEADS * 2 * DIFF_QK
C_DK = DIFF_HEADS * 2 * DIFF_QK
C_DV = DIFF_HEADS * DIFF_V

LANES = 128
HEAD_PAD = 2 * LANES
KEY_CHUNK = 1024
KEY_TILE = 256
SUBLANES = 8
VT_TM = 512
PROJ_TM = 1024
MLA_BQ = 512
DIFF_BQ = 256
VMEM_LIMIT = 56 * 1024 * 1024

F32 = jnp.float32
BF16 = jnp.bfloat16


def _params(sem):
    return pltpu.CompilerParams(dimension_semantics=sem, vmem_limit_bytes=VMEM_LIMIT)


def _rope_mix(x, tab_ref, shift):
    n = x.shape[-1]
    fwd = pltpu.roll(x, n - shift, 1)
    bwd = pltpu.roll(x, shift, 1)
    return x * tab_ref[0] + fwd * tab_ref[1] + bwd * tab_ref[2]


def _rmsnorm_rows(x, g):
    ms = jnp.mean(x * x, axis=-1, keepdims=True)
    return x * lax.rsqrt(ms + RMS_EPS) * g


def _layernorm_rows(z, g, b):
    mu = jnp.mean(z, axis=-1, keepdims=True)
    zc = z - mu
    var = jnp.mean(zc * zc, axis=-1, keepdims=True)
    return zc * lax.rsqrt(var + LN_EPS) * g + b


def _q_proj_kernel(x_ref, wqa_ref, gqa_ref, wqb_ref, tab_ref, qt_ref, *, scale):
    xb = x_ref[...].astype(BF16)
    qa = jnp.dot(xb, wqa_ref[...], preferred_element_type=F32)
    qn = _rmsnorm_rows(qa, gqa_ref[...]).astype(BF16)
    q = jnp.dot(qn, wqb_ref[...], preferred_element_type=F32)
    nope_w = MLA_HEADS * MLA_NOPE
    for h in range(MLA_HEADS):
        nope = q[:, h * MLA_NOPE:(h + 1) * MLA_NOPE]
        rope = _rope_mix(q[:, nope_w + h * LANES: nope_w + (h + 1) * LANES], tab_ref, MLA_ROPE // 2)
        nope_t = (nope * scale).T.astype(BF16)
        rope_t = (rope * scale).T.astype(BF16)
        for blk in range(qt_ref.shape[1]):
            cols = slice(blk * MLA_BQ, (blk + 1) * MLA_BQ)
            qt_ref[h, blk, :LANES, :] = nope_t[:, cols]
            qt_ref[h, blk, LANES:, :] = rope_t[:, cols]


def _kv_proj_kernel(x_ref, wkva_ref, gkva_ref, wkvb_ref, tab_ref, k_ref, vt_ref):
    xb = x_ref[...].astype(BF16)
    kva = jnp.dot(xb, wkva_ref[...], preferred_element_type=F32)
    ckv = _rmsnorm_rows(kva[:, :MLA_KV_LORA], gkva_ref[...]).astype(BF16)
    krope = _rope_mix(kva[:, MLA_KV_LORA:], tab_ref, MLA_ROPE // 2).astype(BF16)
    kv = jnp.dot(ckv, wkvb_ref[...], preferred_element_type=F32)
    nope_w = MLA_HEADS * MLA_NOPE
    for h in range(MLA_HEADS):
        k_ref[:, h * HEAD_PAD: h * HEAD_PAD + LANES] = kv[:, h * MLA_NOPE:(h + 1) * MLA_NOPE].astype(BF16)
        k_ref[:, h * HEAD_PAD + LANES:(h + 1) * HEAD_PAD] = krope
    vt = kv[:, nope_w:].T
    vt_ref[:, 0] = vt.reshape(MLA_HEADS, MLA_V, vt.shape[-1]).astype(BF16)


def _dq_proj_kernel(x_ref, w_ref, tab_ref, qt_ref, *, scale):
    xb = x_ref[...].astype(BF16)
    dq = jnp.dot(xb, w_ref[...], preferred_element_type=F32)
    first_map = lax.broadcasted_iota(jnp.int32, (dq.shape[0], LANES), 1) < DIFF_QK
    for h in range(DIFF_HEADS):
        r = _rope_mix(dq[:, h * LANES:(h + 1) * LANES], tab_ref, DIFF_ROT // 2) * scale
        r1 = jnp.where(first_map, r, 0.0).T.astype(BF16)
        r2 = jnp.where(first_map, 0.0, r).T.astype(BF16)
        for blk in range(qt_ref.shape[1]):
            cols = slice(blk * DIFF_BQ, (blk + 1) * DIFF_BQ)
            qt_ref[h, blk, :, :DIFF_BQ] = r1[:, cols]
            qt_ref[h, blk, :, DIFF_BQ:] = r2[:, cols]


def _dk_proj_kernel(x_ref, w_ref, tab_ref, k_ref):
    xb = x_ref[...].astype(BF16)
    dk = jnp.dot(xb, w_ref[...], preferred_element_type=F32)
    for h in range(DIFF_HEADS):
        k_ref[:, h * LANES:(h + 1) * LANES] = _rope_mix(
            dk[:, h * LANES:(h + 1) * LANES], tab_ref, DIFF_ROT // 2).astype(BF16)


def _dv_proj_kernel(x_ref, w_ref, vt_ref):
    xb = x_ref[...].astype(BF16)
    dv = jnp.dot(xb, w_ref[...], preferred_element_type=F32)
    vt = dv.T
    vt_ref[:, 0] = vt.reshape(DIFF_HEADS, DIFF_V, vt.shape[-1]).astype(BF16)


def _gate_kernel(x_ref, w_ref, b_ref, g_ref):
    xb = x_ref[...].astype(BF16)
    z = jnp.dot(xb, w_ref[...], preferred_element_type=F32) + b_ref[...]
    g_ref[...] = (1.0 / (1.0 + jnp.exp(-z))).astype(g_ref.dtype)


def _row_spec(tm, width):
    return pl.BlockSpec((tm, width), lambda i: (i, 0))


def _full_spec(shape):
    nd = len(shape)
    return pl.BlockSpec(shape, lambda i: (0,) * nd, pipeline_mode=pl.Buffered(1))


def _tab_spec(tm, s_tiles):
    return pl.BlockSpec((3, tm, LANES), lambda i: (0, i % s_tiles, 0))


def _vt_out(t, tm):
    per_chunk = KEY_CHUNK // tm
    shape = jax.ShapeDtypeStruct((MLA_HEADS, t // KEY_CHUNK, MLA_V, KEY_CHUNK), BF16)
    spec = pl.BlockSpec((MLA_HEADS, 1, MLA_V, tm), lambda i: (0, i // per_chunk, 0, i % per_chunk))
    return shape, spec


def _q_proj(x, wqa, gqa, wqb, tab, seq):
    t, tm = x.shape[0], PROJ_TM
    blocks = tm // MLA_BQ
    return pl.pallas_call(
        functools.partial(_q_proj_kernel, scale=LOG2_E * (MLA_NOPE + MLA_ROPE) ** -0.5),
        grid=(t // tm,),
        in_specs=[_row_spec(tm, D_MODEL), _full_spec(wqa.shape), _full_spec(gqa.shape),
                  _full_spec(wqb.shape), _tab_spec(tm, seq // tm)],
        out_specs=pl.BlockSpec((MLA_HEADS, blocks, HEAD_PAD, MLA_BQ), lambda i: (0, i, 0, 0)),
        out_shape=jax.ShapeDtypeStruct((MLA_HEADS, t // MLA_BQ, HEAD_PAD, MLA_BQ), BF16),
        compiler_params=_params(("parallel",)),
        name="mla_q_proj",
    )(x, wqa, gqa, wqb, tab)


def _kv_proj(x, wkva, gkva, wkvb, tab, seq):
    t, tm = x.shape[0], VT_TM
    vt_shape, vt_spec = _vt_out(t, tm)
    return pl.pallas_call(
        _kv_proj_kernel,
        grid=(t // tm,),
        in_specs=[_row_spec(tm, D_MODEL), _full_spec(wkva.shape), _full_spec(gkva.shape),
                  _full_spec(wkvb.shape), _tab_spec(tm, seq // tm)],
        out_specs=[_row_spec(tm, MLA_HEADS * HEAD_PAD), vt_spec],
        out_shape=[jax.ShapeDtypeStruct((t, MLA_HEADS * HEAD_PAD), BF16), vt_shape],
        compiler_params=_params(("parallel",)),
        name="mla_kv_proj",
    )(x, wkva, gkva, wkvb, tab)


def _dq_proj(x, w, tab, seq):
    t, tm = x.shape[0], PROJ_TM
    blocks = tm // DIFF_BQ
    return pl.pallas_call(
        functools.partial(_dq_proj_kernel, scale=LOG2_E * DIFF_QK ** -0.5),
        grid=(t // tm,),
        in_specs=[_row_spec(tm, D_MODEL), _full_spec(w.shape), _tab_spec(tm, seq // tm)],
        out_specs=pl.BlockSpec((DIFF_HEADS, blocks, LANES, 2 * DIFF_BQ), lambda i: (0, i, 0, 0)),
        out_shape=jax.ShapeDtypeStruct((DIFF_HEADS, t // DIFF_BQ, LANES, 2 * DIFF_BQ), BF16),
        compiler_params=_params(("parallel",)),
        name="diff_q_proj",
    )(x, w, tab)


def _dk_proj(x, w, tab, seq):
    t, tm = x.shape[0], PROJ_TM
    return pl.pallas_call(
        _dk_proj_kernel,
        grid=(t // tm,),
        in_specs=[_row_spec(tm, D_MODEL), _full_spec(w.shape), _tab_spec(tm, seq // tm)],
        out_specs=_row_spec(tm, C_DK),
        out_shape=jax.ShapeDtypeStruct((t, C_DK), BF16),
        compiler_params=_params(("parallel",)),
        name="diff_k_proj",
    )(x, w, tab)


def _dv_proj(x, w):
    t, tm = x.shape[0], PROJ_TM
    vt_shape, vt_spec = _vt_out(t, tm)
    return pl.pallas_call(
        _dv_proj_kernel,
        grid=(t // tm,),
        in_specs=[_row_spec(tm, D_MODEL), _full_spec(w.shape)],
        out_specs=vt_spec,
        out_shape=vt_shape,
        compiler_params=_params(("parallel",)),
        name="diff_v_proj",
    )(x, w)


def _gates(x, w, b):
    t, tm = x.shape[0], PROJ_TM
    n = w.shape[1]
    tn = D_MODEL
    return pl.pallas_call(
        _gate_kernel,
        grid=(n // tn, t // tm),
        in_specs=[pl.BlockSpec((tm, D_MODEL), lambda j, i: (i, 0)),
                  pl.BlockSpec((D_MODEL, tn), lambda j, i: (0, j), pipeline_mode=pl.Buffered(1)),
                  pl.BlockSpec((1, tn), lambda j, i: (0, j))],
        out_specs=pl.BlockSpec((tm, tn), lambda j, i: (i, j)),
        out_shape=jax.ShapeDtypeStruct((t, n), BF16),
        compiler_params=_params(("parallel", "parallel")),
        name="gates",
    )(x, w, b)


def _flash_head(qt_ref, k_ref, vt_ref, s_ref, p_ref, m_ref, l_ref, acc_ref):
    nq, _, nc = qt_ref.shape
    n_chunks = vt_ref.shape[0]
    n_steps = n_chunks * nq
    n_tiles = KEY_CHUNK // KEY_TILE
    block_bits = nq.bit_length() - 1
    assert nq == 1 << block_bits and n_steps % 2 == 0

    def chunk_and_block(g):
        return lax.shift_right_logical(g, block_bits), lax.bitwise_and(g, nq - 1)

    def sublane_groups(x):
        return x.reshape(KEY_TILE // SUBLANES, SUBLANES, nc)

    def score_tile(g, t):
        chunk, blk = chunk_and_block(g)
        start = pl.multiple_of(chunk * KEY_CHUNK + t * KEY_TILE, KEY_TILE)
        return jnp.dot(k_ref[pl.ds(start, KEY_TILE), :], qt_ref[blk], preferred_element_type=F32)

    def add_pv(g_prev, alpha_prev):
        chunk, blk = chunk_and_block(g_prev)
        pv = jnp.dot(vt_ref[chunk], p_ref[...], preferred_element_type=F32)
        acc_ref[blk] = alpha_prev * acc_ref[blk] + pv

    def step(g, carry):
        alpha_prev, s_max = carry
        _, blk = chunk_and_block(g)
        m = m_ref[blk]
        m_new = jnp.maximum(m, s_max)
        alpha = jnp.exp2(m - m_new)
        m_ref[blk] = m_new
        add_pv(jnp.maximum(g - 1, 0), alpha_prev)
        g_next = jnp.minimum(g + 1, n_steps - 1)
        next_max = None
        p_sum = None
        for t in range(n_tiles):
            rows = pl.ds(t * KEY_TILE, KEY_TILE)
            p = jnp.exp2(s_ref[rows, :] - m_new)
            p_ref[rows, :] = p.astype(BF16)
            tile_sum = jnp.sum(sublane_groups(p), axis=0)
            p_sum = tile_sum if t == 0 else p_sum + tile_sum
            s_tile = score_tile(g_next, t)
            s_ref[rows, :] = s_tile
            tile_max = jnp.max(sublane_groups(s_tile), axis=0)
            next_max = tile_max if t == 0 else jnp.maximum(next_max, tile_max)
        l_ref[blk] = alpha * l_ref[blk] + jnp.sum(p_sum, axis=0, keepdims=True)
        return alpha, jnp.max(next_max, axis=0, keepdims=True)

    first_max = None
    for t in range(n_tiles):
        s_tile = score_tile(0, t)
        s_ref[pl.ds(t * KEY_TILE, KEY_TILE), :] = s_tile
        tile_max = jnp.max(sublane_groups(s_tile), axis=0)
        first_max = tile_max if t == 0 else jnp.maximum(first_max, tile_max)
    p_ref[...] = jnp.zeros(p_ref.shape, BF16)
    m_ref[...] = jnp.full(m_ref.shape, -jnp.inf, F32)
    l_ref[...] = jnp.zeros(l_ref.shape, F32)
    acc_ref[...] = jnp.zeros(acc_ref.shape, F32)

    init = (jnp.ones((1, nc), F32), jnp.max(first_max, axis=0, keepdims=True))
    alpha_last, _ = lax.fori_loop(0, n_steps, step, init, unroll=2)
    add_pv(n_steps - 1, alpha_last)


def _mla_flash_kernel(qt_ref, k_ref, vt_ref, o_ref, s_ref, p_ref, m_ref, l_ref, acc_ref):
    _flash_head(qt_ref, k_ref, vt_ref, s_ref, p_ref, m_ref, l_ref, acc_ref)
    nq, _, nc = qt_ref.shape

    def finish(blk, _):
        o = acc_ref[blk] * (1.0 / l_ref[blk])
        o_ref[pl.ds(pl.multiple_of(blk * nc, nc), nc), :] = o.T.astype(BF16)
        return 0

    lax.fori_loop(0, nq, finish, 0)


def _diff_flash_kernel(qt_ref, k_ref, vt_ref, lq_ref, lk_ref, gsub_ref, o_ref,
                       s_ref, p_ref, m_ref, l_ref, acc_ref, *, lam_init):
    _flash_head(qt_ref, k_ref, vt_ref, s_ref, p_ref, m_ref, l_ref, acc_ref)
    nq, _, nc = qt_ref.shape
    bq = nc // 2
    lam_dot = jnp.sum(lq_ref[...] * lk_ref[...], axis=-1, keepdims=True)
    lam_exp = jnp.exp(lam_dot)
    lam = lam_exp[0:1, :] - lam_exp[1:2, :] + lam_init

    def finish(blk, _):
        on = acc_ref[blk] * (1.0 / l_ref[blk])
        a = on[:, :bq] - lam * on[:, bq:]
        ms = jnp.mean(a * a, axis=0, keepdims=True)
        y = (a * lax.rsqrt(ms + RMS_EPS)).T * gsub_ref[...]
        o_ref[pl.ds(pl.multiple_of(blk * bq, bq), bq), :] = (y * (1.0 - lam_init)).astype(BF16)
        return 0

    lax.fori_loop(0, nq, finish, 0)


def _flash_call(kernel, qt, k, vt, extra, batch, seq, dk, dv, nc, name):
    heads = qt.shape[0]
    nq = qt.shape[1] // batch
    n_chunks = seq // KEY_CHUNK
    once = pl.Buffered(1)
    small = lambda shape: pl.BlockSpec(shape, lambda b, h: (0, 0))
    return pl.pallas_call(
        kernel,
        grid=(batch, heads),
        in_specs=[pl.BlockSpec((None, nq, dk, nc), lambda b, h: (h, b, 0, 0), pipeline_mode=once),
                  pl.BlockSpec((seq, dk), lambda b, h: (b, h), pipeline_mode=once),
                  pl.BlockSpec((None, n_chunks, dv, KEY_CHUNK), lambda b, h: (h, b, 0, 0),
                               pipeline_mode=once)] + [small(e.shape) for e in extra],
        out_specs=pl.BlockSpec((seq, dv), lambda b, h: (b, h)),
        out_shape=jax.ShapeDtypeStruct((batch * seq, heads * dv), BF16),
        scratch_shapes=[pltpu.VMEM((KEY_CHUNK, nc), F32), pltpu.VMEM((KEY_CHUNK, nc), BF16),
                        pltpu.VMEM((nq, 1, nc), F32), pltpu.VMEM((nq, 1, nc), F32),
                        pltpu.VMEM((nq, dv, nc), F32)],
        compiler_params=_params(("parallel", "parallel")),
        name=name,
    )(qt, k, vt, *extra)


def _mla_flash(qt, k, vt, batch, seq):
    return _flash_call(_mla_flash_kernel, qt, k, vt, (), batch, seq, HEAD_PAD, MLA_V, MLA_BQ, "mla_flash")


def _diff_flash(qt, k, vt, lam_q, lam_k, g_sub, batch, seq, lam_init):
    return _flash_call(functools.partial(_diff_flash_kernel, lam_init=lam_init), qt, k, vt,
                       (lam_q, lam_k, g_sub), batch, seq, LANES, DIFF_V, 2 * DIFF_BQ, "diff_flash")


def _merge_kernel(om_ref, od_ref, gm_ref, gd_ref, wm_ref, wd_ref, o_ref):
    a = jnp.dot(om_ref[...], wm_ref[...], preferred_element_type=F32)
    b = jnp.dot(od_ref[...], wd_ref[...], preferred_element_type=F32)
    o_ref[...] = (gm_ref[...] * a + gd_ref[...] * b).astype(BF16)


def _outproj_ln_kernel(x_ref, m_ref, w_ref, g_ref, b_ref, o_ref):
    h = jnp.dot(m_ref[...], w_ref[...], preferred_element_type=F32)
    o_ref[...] = _layernorm_rows(DN_ALPHA * x_ref[...] + h, g_ref[...], b_ref[...])


def _ffn_ln_kernel(x_ref, w1_ref, w2_ref, g_ref, b_ref, o_ref, xb_ref, acc_ref):
    j = pl.program_id(1)

    @pl.when(j == 0)
    def _():
        xb_ref[...] = x_ref[...].astype(BF16)
        acc_ref[...] = jnp.zeros_like(acc_ref)

    h = jnp.maximum(jnp.dot(xb_ref[...], w1_ref[...], preferred_element_type=F32), 0.0)
    acc_ref[...] += jnp.dot((h * h).astype(BF16), w2_ref[...], preferred_element_type=F32)

    @pl.when(j == pl.num_programs(1) - 1)
    def _():
        o_ref[...] = _layernorm_rows(DN_ALPHA * x_ref[...] + acc_ref[...], g_ref[...], b_ref[...])


def _merge(o_mla, o_diff, gates, w_br_mla, w_br_diff):
    t, tm = o_mla.shape[0], 512
    return pl.pallas_call(
        _merge_kernel,
        grid=(t // tm,),
        in_specs=[_row_spec(tm, o_mla.shape[1]), _row_spec(tm, o_diff.shape[1]),
                  pl.BlockSpec((tm, D_MODEL), lambda i: (i, 0)),
                  pl.BlockSpec((tm, D_MODEL), lambda i: (i, 1)),
                  _full_spec(w_br_mla.shape), _full_spec(w_br_diff.shape)],
        out_specs=_row_spec(tm, D_MODEL),
        out_shape=jax.ShapeDtypeStruct((t, D_MODEL), BF16),
        compiler_params=_params(("parallel",)),
        name="branch_merge",
    )(o_mla, o_diff, gates, gates, w_br_mla, w_br_diff)


def _outproj_ln(x, merged, w_out, g, b):
    t, tm = x.shape[0], 512
    return pl.pallas_call(
        _outproj_ln_kernel,
        grid=(t // tm,),
        in_specs=[_row_spec(tm, D_MODEL), _row_spec(tm, D_MODEL), _full_spec(w_out.shape),
                  _full_spec(g.shape), _full_spec(b.shape)],
        out_specs=_row_spec(tm, D_MODEL),
        out_shape=jax.ShapeDtypeStruct((t, D_MODEL), F32),
        compiler_params=_params(("parallel",)),
        name="outproj_ln",
    )(x, merged, w_out, g, b)


def _ffn_ln(x, w1, w2, g, b):
    t, tm, tf = x.shape[0], 512, 1024
    return pl.pallas_call(
        _ffn_ln_kernel,
        grid=(t // tm, D_FF // tf),
        in_specs=[pl.BlockSpec((tm, D_MODEL), lambda i, j: (i, 0)),
                  pl.BlockSpec((D_MODEL, tf), lambda i, j: (0, j)),
                  pl.BlockSpec((tf, D_MODEL), lambda i, j: (j, 0)),
                  pl.BlockSpec((1, D_MODEL), lambda i, j: (0, 0)),
                  pl.BlockSpec((1, D_MODEL), lambda i, j: (0, 0))],
        out_specs=pl.BlockSpec((tm, D_MODEL), lambda i, j: (i, 0)),
        out_shape=jax.ShapeDtypeStruct((t, D_MODEL), F32),
        scratch_shapes=[pltpu.VMEM((tm, D_MODEL), BF16), pltpu.VMEM((tm, D_MODEL), F32)],
        compiler_params=_params(("parallel", "arbitrary")),
        name="ffn_ln",
    )(x, w1, w2, g, b)


def _rope_tables(seq, rot_dim, group):
    half = rot_dim // 2
    inv_freq = ROPE_THETA ** (-jnp.arange(0, rot_dim, 2, dtype=F32) / rot_dim)
    ang = jnp.arange(seq, dtype=F32)[:, None] * inv_freq[None, :]
    cos, sin = jnp.cos(ang), jnp.sin(ang)
    zeros = lambda n: jnp.zeros((seq, n), F32)
    c = jnp.concatenate([cos, cos, jnp.ones((seq, group - rot_dim), F32)], axis=1)
    s_fwd = jnp.concatenate([-sin, zeros(group - half)], axis=1)
    s_bwd = jnp.concatenate([zeros(half), sin, zeros(group - rot_dim)], axis=1)
    return jnp.stack([c, s_fwd, s_bwd])


def _prep_weights(w_in, b_gate, g_qa, w_qb, g_kva, w_kvb, g_sub, w_br_mla, w_br_diff, w_out,
                  ln1_g, ln1_b, w_ff1, w_ff2, ln2_g, ln2_b, l):
    c0 = C_QA
    c1 = c0 + C_KVA
    c2 = c1 + C_DQ
    c3 = c2 + C_DK
    c4 = c3 + C_DV
    wi = w_in[l]
    qb = w_qb[l].reshape(MLA_Q_LORA, MLA_HEADS, MLA_NOPE + MLA_ROPE)
    qb_nope = qb[:, :, :MLA_NOPE].reshape(MLA_Q_LORA, MLA_HEADS * MLA_NOPE)
    qb_rope = jnp.pad(qb[:, :, MLA_NOPE:], ((0, 0), (0, 0), (0, LANES - MLA_ROPE)))
    qb_rope = qb_rope.reshape(MLA_Q_LORA, MLA_HEADS * LANES)
    kvb = w_kvb[l].reshape(MLA_KV_LORA, MLA_HEADS, MLA_NOPE + MLA_V)
    kvb_k = kvb[:, :, :MLA_NOPE].reshape(MLA_KV_LORA, MLA_HEADS * MLA_NOPE)
    kvb_v = kvb[:, :, MLA_NOPE:].reshape(MLA_KV_LORA, MLA_HEADS * MLA_V)
    row = lambda v: v[l].reshape(1, -1)
    return dict(
        wqa=wi[:, :c0].astype(BF16),
        wkva=jnp.pad(wi[:, c0:c1], ((0, 0), (0, LANES - MLA_ROPE))).astype(BF16),
        wdq=wi[:, c1:c2].astype(BF16),
        wdk=wi[:, c2:c3].astype(BF16),
        wdv=wi[:, c3:c4].astype(BF16),
        wgate=wi[:, c4:].astype(BF16),
        b_gate=row(b_gate),
        g_qa=row(g_qa),
        wqb=jnp.concatenate([qb_nope, qb_rope], axis=1).astype(BF16),
        g_kva=row(g_kva),
        wkvb=jnp.concatenate([kvb_k, kvb_v], axis=1).astype(BF16),
        g_sub=row(g_sub),
        w_br_mla=w_br_mla[l].astype(BF16),
        w_br_diff=w_br_diff[l].astype(BF16),
        w_out=w_out[l].astype(BF16),
        ln1_g=row(ln1_g), ln1_b=row(ln1_b),
        w_ff1=w_ff1[l].astype(BF16), w_ff2=w_ff2[l].astype(BF16),
        ln2_g=row(ln2_g), ln2_b=row(ln2_b),
    )


def _layer(x3, w, lam_q, lam_k, lam_init):
    batch, seq, _ = x3.shape
    x = x3.reshape(batch * seq, D_MODEL)
    tab_mla = jnp.pad(_rope_tables(seq, MLA_ROPE, MLA_ROPE), ((0, 0), (0, 0), (0, LANES - MLA_ROPE)))
    tab_diff = jnp.tile(_rope_tables(seq, DIFF_ROT, DIFF_QK), (1, 1, LANES // DIFF_QK))

    qt = _q_proj(x, w["wqa"], w["g_qa"], w["wqb"], tab_mla, seq)
    k, vt = _kv_proj(x, w["wkva"], w["g_kva"], w["wkvb"], tab_mla, seq)
    o_mla = _mla_flash(qt, k, vt, batch, seq)

    dqt = _dq_proj(x, w["wdq"], tab_diff, seq)
    dk = _dk_proj(x, w["wdk"], tab_diff, seq)
    dvt = _dv_proj(x, w["wdv"])
    o_diff = _diff_flash(dqt, dk, dvt, lam_q, lam_k, w["g_sub"], batch, seq, lam_init)

    gates = _gates(x, w["wgate"], w["b_gate"])
    merged = _merge(o_mla, o_diff, gates, w["w_br_mla"], w["w_br_diff"])
    x1 = _outproj_ln(x, merged, w["w_out"], w["ln1_g"], w["ln1_b"])
    y = _ffn_ln(x1, w["w_ff1"], w["w_ff2"], w["ln2_g"], w["ln2_b"])
    return y.reshape(batch, seq, D_MODEL)


def kernel(x_prompt, x_sample, w_in, b_gate, g_qa, w_qb, g_kva, w_kvb, lam_q, lam_k, g_sub,
           w_br_mla, w_br_diff, w_out, ln1_g, ln1_b, w_ff1, w_ff2, ln2_g, ln2_b):
    outs = [x_prompt, x_sample]
    for l in range(DEPTH):
        lam_init = 0.8 - 0.6 * math.exp(-0.3 * l)
        w = _prep_weights(w_in, b_gate, g_qa, w_qb, g_kva, w_kvb, g_sub, w_br_mla, w_br_diff, w_out,
                          ln1_g, ln1_b, w_ff1, w_ff2, ln2_g, ln2_b, l)
        outs = [_layer(x3, w, lam_q[l], lam_k[l], lam_init) for x3 in outs]
    return tuple(outs)
```

```python
import functools
import math

import jax
import jax.numpy as jnp
from jax import lax
from jax.experimental import pallas as pl
from jax.experimental.pallas import tpu as pltpu

D_MODEL = 2048
DEPTH = 1
MLA_HEADS = 8
MLA_Q_LORA = 768
MLA_KV_LORA = 512
MLA_NOPE = 128
MLA_ROPE = 64
MLA_V = 128
DIFF_HEADS = 8
DIFF_QK = 64
DIFF_V = 2 * DIFF_QK
DIFF_ROT = DIFF_QK // 4
D_FF = 4 * D_MODEL
ROPE_THETA = 500000.0
LN_EPS = 1e-5
RMS_EPS = 1e-6
DN_ALPHA = (2.0 * DEPTH) ** 0.25
LOG2_E = math.log2(math.e)

C_QA = MLA_Q_LORA
C_KVA = MLA_KV_LORA + MLA_ROPE
C_DQ = DIFF_HEADS * 2 * DIFF_QK
C_DK = DIFF_HEADS * 2 * DIFF_QK
C_DV = DIFF_HEADS * DIFF_V

LANES = 128
HEAD_PAD = 2 * LANES
KEY_CHUNK = 1024
KEY_TILE = 512
SUBLANES = 8
VT_TM = 512
PROJ_TM = 1024
MLA_BQ = 512
DIFF_BQ = 256
VMEM_LIMIT = 56 * 1024 * 1024

F32 = jnp.float32
BF16 = jnp.bfloat16


def _params(sem):
    return pltpu.CompilerParams(dimension_semantics=sem, vmem_limit_bytes=VMEM_LIMIT)


def _rope_mix(x, tab_ref, shift):
    n = x.shape[-1]
    fwd = pltpu.roll(x, n - shift, 1)
    bwd = pltpu.roll(x, shift, 1)
    return x * tab_ref[0] + fwd * tab_ref[1] + bwd * tab_ref[2]


def _rmsnorm_rows(x, g):
    ms = jnp.mean(x * x, axis=-1, keepdims=True)
    return x * lax.rsqrt(ms + RMS_EPS) * g


def _layernorm_rows(z, g, b):
    mu = jnp.mean(z, axis=-1, keepdims=True)
    zc = z - mu
    var = jnp.mean(zc * zc, axis=-1, keepdims=True)
    return zc * lax.rsqrt(var + LN_EPS) * g + b


def _q_proj_kernel(x_ref, wqa_ref, gqa_ref, wqb_ref, tab_ref, qt_ref, *, scale):
    xb = x_ref[...].astype(BF16)
    qa = jnp.dot(xb, wqa_ref[...], preferred_element_type=F32)
    qn = _rmsnorm_rows(qa, gqa_ref[...]).astype(BF16)
    q = jnp.dot(qn, wqb_ref[...], preferred_element_type=F32)
    nope_w = MLA_HEADS * MLA_NOPE
    for h in range(MLA_HEADS):
        nope = q[:, h * MLA_NOPE:(h + 1) * MLA_NOPE]
        rope = _rope_mix(q[:, nope_w + h * LANES: nope_w + (h + 1) * LANES], tab_ref, MLA_ROPE // 2)
        nope_t = (nope * scale).T.astype(BF16)
        rope_t = (rope * scale).T.astype(BF16)
        for blk in range(qt_ref.shape[1]):
            cols = slice(blk * MLA_BQ, (blk + 1) * MLA_BQ)
            qt_ref[h, blk, :LANES, :] = nope_t[:, cols]
            qt_ref[h, blk, LANES:, :] = rope_t[:, cols]


def _kv_proj_kernel(x_ref, wkva_ref, gkva_ref, wkvb_ref, tab_ref, k_ref, vt_ref):
    xb = x_ref[...].astype(BF16)
    kva = jnp.dot(xb, wkva_ref[...], preferred_element_type=F32)
    ckv = _rmsnorm_rows(kva[:, :MLA_KV_LORA], gkva_ref[...]).astype(BF16)
    krope = _rope_mix(kva[:, MLA_KV_LORA:], tab_ref, MLA_ROPE // 2).astype(BF16)
    kv = jnp.dot(ckv, wkvb_ref[...], preferred_element_type=F32)
    nope_w = MLA_HEADS * MLA_NOPE
    for h in range(MLA_HEADS):
        k_ref[:, h * HEAD_PAD: h * HEAD_PAD + LANES] = kv[:, h * MLA_NOPE:(h + 1) * MLA_NOPE].astype(BF16)
        k_ref[:, h * HEAD_PAD + LANES:(h + 1) * HEAD_PAD] = krope
    vt = kv[:, nope_w:].T
    vt_ref[:, 0] = vt.reshape(MLA_HEADS, MLA_V, vt.shape[-1]).astype(BF16)


def _dq_proj_kernel(x_ref, w_ref, tab_ref, qt_ref, *, scale):
    xb = x_ref[...].astype(BF16)
    dq = jnp.dot(xb, w_ref[...], preferred_element_type=F32)
    first_map = lax.broadcasted_iota(jnp.int32, (dq.shape[0], LANES), 1) < DIFF_QK
    for h in range(DIFF_HEADS):
        r = _rope_mix(dq[:, h * LANES:(h + 1) * LANES], tab_ref, DIFF_ROT // 2) * scale
        r1 = jnp.where(first_map, r, 0.0).T.astype(BF16)
        r2 = jnp.where(first_map, 0.0, r).T.astype(BF16)
        for blk in range(qt_ref.shape[1]):
            cols = slice(blk * DIFF_BQ, (blk + 1) * DIFF_BQ)
            qt_ref[h, blk, :, :DIFF_BQ] = r1[:, cols]
            qt_ref[h, blk, :, DIFF_BQ:] = r2[:, cols]


def _dk_proj_kernel(x_ref, w_ref, tab_ref, k_ref):
    xb = x_ref[...].astype(BF16)
    dk = jnp.dot(xb, w_ref[...], preferred_element_type=F32)
    for h in range(DIFF_HEADS):
        k_ref[:, h * LANES:(h + 1) * LANES] = _rope_mix(
            dk[:, h * LANES:(h + 1) * LANES], tab_ref, DIFF_ROT // 2).astype(BF16)


def _dv_proj_kernel(x_ref, w_ref, vt_ref):
    xb = x_ref[...].astype(BF16)
    dv = jnp.dot(xb, w_ref[...], preferred_element_type=F32)
    vt = dv.T
    vt_ref[:, 0] = vt.reshape(DIFF_HEADS, DIFF_V, vt.shape[-1]).astype(BF16)


def _gate_kernel(x_ref, w_ref, b_ref, g_ref):
    xb = x_ref[...].astype(BF16)
    z = jnp.dot(xb, w_ref[...], preferred_element_type=F32) + b_ref[...]
    g_ref[...] = (1.0 / (1.0 + jnp.exp(-z))).astype(g_ref.dtype)


def _row_spec(tm, width):
    return pl.BlockSpec((tm, width), lambda i: (i, 0))


def _full_spec(shape):
    nd = len(shape)
    return pl.BlockSpec(shape, lambda i: (0,) * nd, pipeline_mode=pl.Buffered(1))


def _tab_spec(tm, s_tiles):
    return pl.BlockSpec((3, tm, LANES), lambda i: (0, i % s_tiles, 0))


def _vt_out(t, tm):
    per_chunk = KEY_CHUNK // tm
    shape = jax.ShapeDtypeStruct((MLA_HEADS, t // KEY_CHUNK, MLA_V, KEY_CHUNK), BF16)
    spec = pl.BlockSpec((MLA_HEADS, 1, MLA_V, tm), lambda i: (0, i // per_chunk, 0, i % per_chunk))
    return shape, spec


def _q_proj(x, wqa, gqa, wqb, tab, seq):
    t, tm = x.shape[0], PROJ_TM
    blocks = tm // MLA_BQ
    return pl.pallas_call(
        functools.partial(_q_proj_kernel, scale=LOG2_E * (MLA_NOPE + MLA_ROPE) ** -0.5),
        grid=(t // tm,),
        in_specs=[_row_spec(tm, D_MODEL), _full_spec(wqa.shape), _full_spec(gqa.shape),
                  _full_spec(wqb.shape), _tab_spec(tm, seq // tm)],
        out_specs=pl.BlockSpec((MLA_HEADS, blocks, HEAD_PAD, MLA_BQ), lambda i: (0, i, 0, 0)),
        out_shape=jax.ShapeDtypeStruct((MLA_HEADS, t // MLA_BQ, HEAD_PAD, MLA_BQ), BF16),
        compiler_params=_params(("parallel",)),
        name="mla_q_proj",
    )(x, wqa, gqa, wqb, tab)


def _kv_proj(x, wkva, gkva, wkvb, tab, seq):
    t, tm = x.shape[0], VT_TM
    vt_shape, vt_spec = _vt_out(t, tm)
    return pl.pallas_call(
        _kv_proj_kernel,
        grid=(t // tm,),
        in_specs=[_row_spec(tm, D_MODEL), _full_spec(wkva.shape), _full_spec(gkva.shape),
                  _full_spec(wkvb.shape), _tab_spec(tm, seq // tm)],
        out_specs=[_row_spec(tm, MLA_HEADS * HEAD_PAD), vt_spec],
        out_shape=[jax.ShapeDtypeStruct((t, MLA_HEADS * HEAD_PAD), BF16), vt_shape],
        compiler_params=_params(("parallel",)),
        name="mla_kv_proj",
    )(x, wkva, gkva, wkvb, tab)


def _dq_proj(x, w, tab, seq):
    t, tm = x.shape[0], PROJ_TM
    blocks = tm // DIFF_BQ
    return pl.pallas_call(
        functools.partial(_dq_proj_kernel, scale=LOG2_E * DIFF_QK ** -0.5),
        grid=(t // tm,),
        in_specs=[_row_spec(tm, D_MODEL), _full_spec(w.shape), _tab_spec(tm, seq // tm)],
        out_specs=pl.BlockSpec((DIFF_HEADS, blocks, LANES, 2 * DIFF_BQ), lambda i: (0, i, 0, 0)),
        out_shape=jax.ShapeDtypeStruct((DIFF_HEADS, t // DIFF_BQ, LANES, 2 * DIFF_BQ), BF16),
        compiler_params=_params(("parallel",)),
        name="diff_q_proj",
    )(x, w, tab)


def _dk_proj(x, w, tab, seq):
    t, tm = x.shape[0], PROJ_TM
    return pl.pallas_call(
        _dk_proj_kernel,
        grid=(t // tm,),
        in_specs=[_row_spec(tm, D_MODEL), _full_spec(w.shape), _tab_spec(tm, seq // tm)],
        out_specs=_row_spec(tm, C_DK),
        out_shape=jax.ShapeDtypeStruct((t, C_DK), BF16),
        compiler_params=_params(("parallel",)),
        name="diff_k_proj",
    )(x, w, tab)


def _dv_proj(x, w):
    t, tm = x.shape[0], PROJ_TM
    vt_shape, vt_spec = _vt_out(t, tm)
    return pl.pallas_call(
        _dv_proj_kernel,
        grid=(t // tm,),
        in_specs=[_row_spec(tm, D_MODEL), _full_spec(w.shape)],
        out_specs=vt_spec,
        out_shape=vt_shape,
        compiler_params=_params(("parallel",)),
        name="diff_v_proj",
    )(x, w)


def _gates(x, w, b):
    t, tm = x.shape[0], PROJ_TM
    n = w.shape[1]
    tn = D_MODEL
    return pl.pallas_call(
        _gate_kernel,
        grid=(n // tn, t // tm),
        in_specs=[pl.BlockSpec((tm, D_MODEL), lambda j, i: (i, 0)),
                  pl.BlockSpec((D_MODEL, tn), lambda j, i: (0, j), pipeline_mode=pl.Buffered(1)),
                  pl.BlockSpec((1, tn), lambda j, i: (0, j))],
        out_specs=pl.BlockSpec((tm, tn), lambda j, i: (i, j)),
        out_shape=jax.ShapeDtypeStruct((t, n), BF16),
        compiler_params=_params(("parallel", "parallel")),
        name="gates",
    )(x, w, b)


def _flash_head(qt_ref, k_ref, vt_ref, s_ref, p_ref, m_ref, l_ref, acc_ref):
    nq, _, nc = qt_ref.shape
    n_chunks = vt_ref.shape[0]
    n_steps = n_chunks * nq
    n_tiles = KEY_CHUNK // KEY_TILE
    block_bits = nq.bit_length() - 1
    assert nq == 1 << block_bits and n_steps % 2 == 0

    def chunk_and_block(g):
        return lax.shift_right_logical(g, block_bits), lax.bitwise_and(g, nq - 1)

    def sublane_groups(x):
        return x.reshape(KEY_TILE // SUBLANES, SUBLANES, nc)

    def score_tile(g, t):
        chunk, blk = chunk_and_block(g)
        start = pl.multiple_of(chunk * KEY_CHUNK + t * KEY_TILE, KEY_TILE)
        return jnp.dot(k_ref[pl.ds(start, KEY_TILE), :], qt_ref[blk], preferred_element_type=F32)

    def add_pv(g_prev, alpha_prev):
        chunk, blk = chunk_and_block(g_prev)
        pv = jnp.dot(vt_ref[chunk], p_ref[...], preferred_element_type=F32)
        acc_ref[blk] = alpha_prev * acc_ref[blk] + pv

    def step(g, carry):
        alpha_prev, s_max = carry
        _, blk = chunk_and_block(g)
        m = m_ref[blk]
        m_new = jnp.maximum(m, s_max)
        alpha = jnp.exp2(m - m_new)
        m_ref[blk] = m_new
        add_pv(jnp.maximum(g - 1, 0), alpha_prev)
        g_next = jnp.minimum(g + 1, n_steps - 1)
        next_max = None
        p_sum = None
        for t in range(n_tiles):
            rows = pl.ds(t * KEY_TILE, KEY_TILE)
            p = jnp.exp2(s_ref[rows, :] - m_new)
            p_ref[rows, :] = p.astype(BF16)
            tile_sum = jnp.sum(sublane_groups(p), axis=0)
            p_sum = tile_sum if t == 0 else p_sum + tile_sum
            s_tile = score_tile(g_next, t)
            s_ref[rows, :] = s_tile
            tile_max = jnp.max(sublane_groups(s_tile), axis=0)
            next_max = tile_max if t == 0 else jnp.maximum(next_max, tile_max)
        l_ref[blk] = alpha * l_ref[blk] + jnp.sum(p_sum, axis=0, keepdims=True)
        return alpha, jnp.max(next_max, axis=0, keepdims=True)

    first_max = None
    for t in range(n_tiles):
        s_tile = score_tile(0, t)
        s_ref[pl.ds(t * KEY_TILE, KEY_TILE), :] = s_tile
        tile_max = jnp.max(sublane_groups(s_tile), axis=0)
        first_max = tile_max if t == 0 else jnp.maximum(first_max, tile_max)
    p_ref[...] = jnp.zeros(p_ref.shape, BF16)
    m_ref[...] = jnp.full(m_ref.shape, -jnp.inf, F32)
    l_ref[...] = jnp.zeros(l_ref.shape, F32)
    acc_ref[...] = jnp.zeros(acc_ref.shape, F32)

    init = (jnp.ones((1, nc), F32), jnp.max(first_max, axis=0, keepdims=True))
    alpha_last, _ = lax.fori_loop(0, n_steps, step, init, unroll=2)
    add_pv(n_steps - 1, alpha_last)


def _mla_flash_kernel(qt_ref, k_ref, vt_ref, o_ref, s_ref, p_ref, m_ref, l_ref, acc_ref):
    _flash_head(qt_ref, k_ref, vt_ref, s_ref, p_ref, m_ref, l_ref, acc_ref)
    nq, _, nc = qt_ref.shape

    def finish(blk, _):
        o = acc_ref[blk] * (1.0 / l_ref[blk])
        o_ref[pl.ds(pl.multiple_of(blk * nc, nc), nc), :] = o.T.astype(BF16)
        return 0

    lax.fori_loop(0, nq, finish, 0)


def _diff_flash_kernel(qt_ref, k_ref, vt_ref, lq_ref, lk_ref, gsub_ref, o_ref,
                       s_ref, p_ref, m_ref, l_ref, acc_ref, *, lam_init):
    _flash_head(qt_ref, k_ref, vt_ref, s_ref, p_ref, m_ref, l_ref, acc_ref)
    nq, _, nc = qt_ref.shape
    bq = nc // 2
    lam_dot = jnp.sum(lq_ref[...] * lk_ref[...], axis=-1, keepdims=True)
    lam_exp = jnp.exp(lam_dot)
    lam = lam_exp[0:1, :] - lam_exp[1:2, :] + lam_init

    def finish(blk, _):
        on = acc_ref[blk] * (1.0 / l_ref[blk])
        a = on[:, :bq] - lam * on[:, bq:]
        ms = jnp.mean(a * a, axis=0, keepdims=True)
        y = (a * lax.rsqrt(ms + RMS_EPS)).T * gsub_ref[...]
        o_ref[pl.ds(pl.multiple_of(blk * bq, bq), bq), :] = (y * (1.0 - lam_init)).astype(BF16)
        return 0

    lax.fori_loop(0, nq, finish, 0)


def _flash_call(kernel, qt, k, vt, extra, batch, seq, dk, dv, nc, name):
    heads = qt.shape[0]
    nq = qt.shape[1] // batch
    n_chunks = seq // KEY_CHUNK
    once = pl.Buffered(1)
    small = lambda shape: pl.BlockSpec(shape, lambda b, h: (0, 0))
    return pl.pallas_call(
        kernel,
        grid=(batch, heads),
        in_specs=[pl.BlockSpec((None, nq, dk, nc), lambda b, h: (h, b, 0, 0), pipeline_mode=once),
                  pl.BlockSpec((seq, dk), lambda b, h: (b, h), pipeline_mode=once),
                  pl.BlockSpec((None, n_chunks, dv, KEY_CHUNK), lambda b, h: (h, b, 0, 0),
                               pipeline_mode=once)] + [small(e.shape) for e in extra],
        out_specs=pl.BlockSpec((seq, dv), lambda b, h: (b, h)),
        out_shape=jax.ShapeDtypeStruct((batch * seq, heads * dv), BF16),
        scratch_shapes=[pltpu.VMEM((KEY_CHUNK, nc), F32), pltpu.VMEM((KEY_CHUNK, nc), BF16),
                        pltpu.VMEM((nq, 1, nc), F32), pltpu.VMEM((nq, 1, nc), F32),
                        pltpu.VMEM((nq, dv, nc), F32)],
        compiler_params=_params(("parallel", "parallel")),
        name=name,
    )(qt, k, vt, *extra)


def _mla_flash(qt, k, vt, batch, seq):
    return _flash_call(_mla_flash_kernel, qt, k, vt, (), batch, seq, HEAD_PAD, MLA_V, MLA_BQ, "mla_flash")


def _diff_flash(qt, k, vt, lam_q, lam_k, g_sub, batch, seq, lam_init):
    return _flash_call(functools.partial(_diff_flash_kernel, lam_init=lam_init), qt, k, vt,
                       (lam_q, lam_k, g_sub), batch, seq, LANES, DIFF_V, 2 * DIFF_BQ, "diff_flash")


def _merge_kernel(om_ref, od_ref, gm_ref, gd_ref, wm_ref, wd_ref, o_ref):
    a = jnp.dot(om_ref[...], wm_ref[...], preferred_element_type=F32)
    b = jnp.dot(od_ref[...], wd_ref[...], preferred_element_type=F32)
    o_ref[...] = (gm_ref[...] * a + gd_ref[...] * b).astype(BF16)


def _outproj_ln_kernel(x_ref, m_ref, w_ref, g_ref, b_ref, o_ref):
    h = jnp.dot(m_ref[...], w_ref[...], preferred_element_type=F32)
    o_ref[...] = _layernorm_rows(DN_ALPHA * x_ref[...] + h, g_ref[...], b_ref[...])


def _ffn_ln_kernel(x_ref, w1_ref, w2_ref, g_ref, b_ref, o_ref, xb_ref, acc_ref):
    j = pl.program_id(1)

    @pl.when(j == 0)
    def _():
        xb_ref[...] = x_ref[...].astype(BF16)
        acc_ref[...] = jnp.zeros_like(acc_ref)

    h = jnp.maximum(jnp.dot(xb_ref[...], w1_ref[...], preferred_element_type=F32), 0.0)
    acc_ref[...] += jnp.dot((h * h).astype(BF16), w2_ref[...], preferred_element_type=F32)

    @pl.when(j == pl.num_programs(1) - 1)
    def _():
        o_ref[...] = _layernorm_rows(DN_ALPHA * x_ref[...] + acc_ref[...], g_ref[...], b_ref[...])


def _merge(o_mla, o_diff, gates, w_br_mla, w_br_diff):
    t, tm = o_mla.shape[0], 512
    return pl.pallas_call(
        _merge_kernel,
        grid=(t // tm,),
        in_specs=[_row_spec(tm, o_mla.shape[1]), _row_spec(tm, o_diff.shape[1]),
                  pl.BlockSpec((tm, D_MODEL), lambda i: (i, 0)),
                  pl.BlockSpec((tm, D_MODEL), lambda i: (i, 1)),
                  _full_spec(w_br_mla.shape), _full_spec(w_br_diff.shape)],
        out_specs=_row_spec(tm, D_MODEL),
        out_shape=jax.ShapeDtypeStruct((t, D_MODEL), BF16),
        compiler_params=_params(("parallel",)),
        name="branch_merge",
    )(o_mla, o_diff, gates, gates, w_br_mla, w_br_diff)


def _outproj_ln(x, merged, w_out, g, b):
    t, tm = x.shape[0], 512
    return pl.pallas_call(
        _outproj_ln_kernel,
        grid=(t // tm,),
        in_specs=[_row_spec(tm, D_MODEL), _row_spec(tm, D_MODEL), _full_spec(w_out.shape),
                  _full_spec(g.shape), _full_spec(b.shape)],
        out_specs=_row_spec(tm, D_MODEL),
        out_shape=jax.ShapeDtypeStruct((t, D_MODEL), F32),
        compiler_params=_params(("parallel",)),
        name="outproj_ln",
    )(x, merged, w_out, g, b)


def _ffn_ln(x, w1, w2, g, b):
    t, tm, tf = x.shape[0], 512, 1024
    return pl.pallas_call(
        _ffn_ln_kernel,
        grid=(t // tm, D_FF // tf),
        in_specs=[pl.BlockSpec((tm, D_MODEL), lambda i, j: (i, 0)),
                  pl.BlockSpec((D_MODEL, tf), lambda i, j: (0, j)),
                  pl.BlockSpec((tf, D_MODEL), lambda i, j: (j, 0)),
                  pl.BlockSpec((1, D_MODEL), lambda i, j: (0, 0)),
                  pl.BlockSpec((1, D_MODEL), lambda i, j: (0, 0))],
        out_specs=pl.BlockSpec((tm, D_MODEL), lambda i, j: (i, 0)),
        out_shape=jax.ShapeDtypeStruct((t, D_MODEL), F32),
        scratch_shapes=[pltpu.VMEM((tm, D_MODEL), BF16), pltpu.VMEM((tm, D_MODEL), F32)],
        compiler_params=_params(("parallel", "arbitrary")),
        name="ffn_ln",
    )(x, w1, w2, g, b)


def _rope_tables(seq, rot_dim, group):
    half = rot_dim // 2
    inv_freq = ROPE_THETA ** (-jnp.arange(0, rot_dim, 2, dtype=F32) / rot_dim)
    ang = jnp.arange(seq, dtype=F32)[:, None] * inv_freq[None, :]
    cos, sin = jnp.cos(ang), jnp.sin(ang)
    zeros = lambda n: jnp.zeros((seq, n), F32)
    c = jnp.concatenate([cos, cos, jnp.ones((seq, group - rot_dim), F32)], axis=1)
    s_fwd = jnp.concatenate([-sin, zeros(group - half)], axis=1)
    s_bwd = jnp.concatenate([zeros(half), sin, zeros(group - rot_dim)], axis=1)
    return jnp.stack([c, s_fwd, s_bwd])


def _prep_weights(w_in, b_gate, g_qa, w_qb, g_kva, w_kvb, g_sub, w_br_mla, w_br_diff, w_out,
                  ln1_g, ln1_b, w_ff1, w_ff2, ln2_g, ln2_b, l):
    c0 = C_QA
    c1 = c0 + C_KVA
    c2 = c1 + C_DQ
    c3 = c2 + C_DK
    c4 = c3 + C_DV
    wi = w_in[l]
    qb = w_qb[l].reshape(MLA_Q_LORA, MLA_HEADS, MLA_NOPE + MLA_ROPE)
    qb_nope = qb[:, :, :MLA_NOPE].reshape(MLA_Q_LORA, MLA_HEADS * MLA_NOPE)
    qb_rope = jnp.pad(qb[:, :, MLA_NOPE:], ((0, 0), (0, 0), (0, LANES - MLA_ROPE)))
    qb_rope = qb_rope.reshape(MLA_Q_LORA, MLA_HEADS * LANES)
    kvb = w_kvb[l].reshape(MLA_KV_LORA, MLA_HEADS, MLA_NOPE + MLA_V)
    kvb_k = kvb[:, :, :MLA_NOPE].reshape(MLA_KV_LORA, MLA_HEADS * MLA_NOPE)
    kvb_v = kvb[:, :, MLA_NOPE:].reshape(MLA_KV_LORA, MLA_HEADS * MLA_V)
    row = lambda v: v[l].reshape(1, -1)
    return dict(
        wqa=wi[:, :c0].astype(BF16),
        wkva=jnp.pad(wi[:, c0:c1], ((0, 0), (0, LANES - MLA_ROPE))).astype(BF16),
        wdq=wi[:, c1:c2].astype(BF16),
        wdk=wi[:, c2:c3].astype(BF16),
        wdv=wi[:, c3:c4].astype(BF16),
        wgate=wi[:, c4:].astype(BF16),
        b_gate=row(b_gate),
        g_qa=row(g_qa),
        wqb=jnp.concatenate([qb_nope, qb_rope], axis=1).astype(BF16),
        g_kva=row(g_kva),
        wkvb=jnp.concatenate([kvb_k, kvb_v], axis=1).astype(BF16),
        g_sub=row(g_sub),
        w_br_mla=w_br_mla[l].astype(BF16),
        w_br_diff=w_br_diff[l].astype(BF16),
        w_out=w_out[l].astype(BF16),
        ln1_g=row(ln1_g), ln1_b=row(ln1_b),
        w_ff1=w_ff1[l].astype(BF16), w_ff2=w_ff2[l].astype(BF16),
        ln2_g=row(ln2_g), ln2_b=row(ln2_b),
    )


def _layer(x3, w, tab_mla, tab_diff, lam_q, lam_k, lam_init):
    batch, seq, _ = x3.shape
    x = x3.reshape(batch * seq, D_MODEL)

    qt = _q_proj(x, w["wqa"], w["g_qa"], w["wqb"], tab_mla, seq)
    k, vt = _kv_proj(x, w["wkva"], w["g_kva"], w["wkvb"], tab_mla, seq)
    o_mla = _mla_flash(qt, k, vt, batch, seq)

    dqt = _dq_proj(x, w["wdq"], tab_diff, seq)
    dk = _dk_proj(x, w["wdk"], tab_diff, seq)
    dvt = _dv_proj(x, w["wdv"])
    o_diff = _diff_flash(dqt, dk, dvt, lam_q, lam_k, w["g_sub"], batch, seq, lam_init)

    gates = _gates(x, w["wgate"], w["b_gate"])
    merged = _merge(o_mla, o_diff, gates, w["w_br_mla"], w["w_br_diff"])
    x1 = _outproj_ln(x, merged, w["w_out"], w["ln1_g"], w["ln1_b"])
    y = _ffn_ln(x1, w["w_ff1"], w["w_ff2"], w["ln2_g"], w["ln2_b"])
    return y.reshape(batch, seq, D_MODEL)


def kernel(x_prompt, x_sample, w_in, b_gate, g_qa, w_qb, g_kva, w_kvb, lam_q, lam_k, g_sub,
           w_br_mla, w_br_diff, w_out, ln1_g, ln1_b, w_ff1, w_ff2, ln2_g, ln2_b):
    outs = [x_prompt, x_sample]
    max_seq = max(x3.shape[1] for x3 in outs)
    tab_mla = jnp.pad(_rope_tables(max_seq, MLA_ROPE, MLA_ROPE), ((0, 0), (0, 0), (0, LANES - MLA_ROPE)))
    tab_diff = jnp.tile(_rope_tables(max_seq, DIFF_ROT, DIFF_QK), (1, 1, LANES // DIFF_QK))
    for l in range(DEPTH):
        lam_init = 0.8 - 0.6 * math.exp(-0.3 * l)
        w = _prep_weights(w_in, b_gate, g_qa, w_qb, g_kva, w_kvb, g_sub, w_br_mla, w_br_diff, w_out,
                          ln1_g, ln1_b, w_ff1, w_ff2, ln2_g, ln2_b, l)
        outs = [_layer(x3, w, tab_mla, tab_diff, lam_q[l], lam_k[l], lam_init) for x3 in outs]
    return tuple(outs)
```

```python
import functools
import math

import jax
import jax.numpy as jnp
from jax import lax
from jax.experimental import pallas as pl
from jax.experimental.pallas import tpu as pltpu

D_MODEL = 2048
DEPTH = 1
MLA_HEADS = 8
MLA_Q_LORA = 768
MLA_KV_LORA = 512
MLA_NOPE = 128
MLA_ROPE = 64
MLA_V = 128
DIFF_HEADS = 8
DIFF_QK = 64
DIFF_V = 2 * DIFF_QK
DIFF_ROT = DIFF_QK // 4
D_FF = 4 * D_MODEL
ROPE_THETA = 500000.0
LN_EPS = 1e-5
RMS_EPS = 1e-6
DN_ALPHA = (2.0 * DEPTH) ** 0.25
LOG2_E = math.log2(math.e)

C_QA = MLA_Q_LORA
C_KVA = MLA_KV_LORA + MLA_ROPE
C_DQ = DIFF_HEADS * 2 * DIFF_QK
C_DK = DIFF_HEADS * 2 * DIFF_QK
C_DV = DIFF_HEADS * DIFF_V

LANES = 128
HEAD_PAD = 2 * LANES
MLA_KEY_CHUNK = 2048
DIFF_KEY_CHUNK = 2048
KEY_TILE = 512
PV_SPAN = 1024
SUBLANES = 8
VT_TM = 512
PROJ_TM = 1024
MLA_BQ = 512
DIFF_BQ = 256
VMEM_LIMIT = 56 * 1024 * 1024

F32 = jnp.float32
BF16 = jnp.bfloat16


def _params(sem):
    return pltpu.CompilerParams(dimension_semantics=sem, vmem_limit_bytes=VMEM_LIMIT)


def _rope_mix(x, tab_ref, shift):
    n = x.shape[-1]
    fwd = pltpu.roll(x, n - shift, 1)
    bwd = pltpu.roll(x, shift, 1)
    return x * tab_ref[0] + fwd * tab_ref[1] + bwd * tab_ref[2]


def _rmsnorm_rows(x, g):
    ms = jnp.mean(x * x, axis=-1, keepdims=True)
    return x * lax.rsqrt(ms + RMS_EPS) * g


def _layernorm_rows(z, g, b):
    mu = jnp.mean(z, axis=-1, keepdims=True)
    zc = z - mu
    var = jnp.mean(zc * zc, axis=-1, keepdims=True)
    return zc * lax.rsqrt(var + LN_EPS) * g + b


def _q_proj_kernel(x_ref, wqa_ref, gqa_ref, wqb_ref, tab_ref, qt_ref, *, scale):
    xb = x_ref[...].astype(BF16)
    qa = jnp.dot(xb, wqa_ref[...], preferred_element_type=F32)
    qn = _rmsnorm_rows(qa, gqa_ref[...]).astype(BF16)
    q = jnp.dot(qn, wqb_ref[...], preferred_element_type=F32)
    nope_w = MLA_HEADS * MLA_NOPE
    for h in range(MLA_HEADS):
        nope = q[:, h * MLA_NOPE:(h + 1) * MLA_NOPE]
        rope = _rope_mix(q[:, nope_w + h * LANES: nope_w + (h + 1) * LANES], tab_ref, MLA_ROPE // 2)
        nope_t = (nope * scale).T.astype(BF16)
        rope_t = (rope * scale).T.astype(BF16)
        for blk in range(qt_ref.shape[1]):
            cols = slice(blk * MLA_BQ, (blk + 1) * MLA_BQ)
            qt_ref[h, blk, :LANES, :] = nope_t[:, cols]
            qt_ref[h, blk, LANES:, :] = rope_t[:, cols]


def _kv_proj_kernel(x_ref, wkva_ref, gkva_ref, wkvb_ref, tab_ref, k_ref, vt_ref):
    xb = x_ref[...].astype(BF16)
    kva = jnp.dot(xb, wkva_ref[...], preferred_element_type=F32)
    ckv = _rmsnorm_rows(kva[:, :MLA_KV_LORA], gkva_ref[...]).astype(BF16)
    krope = _rope_mix(kva[:, MLA_KV_LORA:], tab_ref, MLA_ROPE // 2).astype(BF16)
    kv = jnp.dot(ckv, wkvb_ref[...], preferred_element_type=F32)
    nope_w = MLA_HEADS * MLA_NOPE
    for h in range(MLA_HEADS):
        k_ref[:, h * HEAD_PAD: h * HEAD_PAD + LANES] = kv[:, h * MLA_NOPE:(h + 1) * MLA_NOPE].astype(BF16)
        k_ref[:, h * HEAD_PAD + LANES:(h + 1) * HEAD_PAD] = krope
    vt = kv[:, nope_w:].T
    vt_ref[:, 0] = vt.reshape(MLA_HEADS, MLA_V, vt.shape[-1]).astype(BF16)


def _dq_proj_kernel(x_ref, w_ref, tab_ref, qt_ref, *, scale):
    xb = x_ref[...].astype(BF16)
    dq = jnp.dot(xb, w_ref[...], preferred_element_type=F32)
    first_map = lax.broadcasted_iota(jnp.int32, (dq.shape[0], LANES), 1) < DIFF_QK
    for h in range(DIFF_HEADS):
        r = _rope_mix(dq[:, h * LANES:(h + 1) * LANES], tab_ref, DIFF_ROT // 2) * scale
        r1 = jnp.where(first_map, r, 0.0).T.astype(BF16)
        r2 = jnp.where(first_map, 0.0, r).T.astype(BF16)
        for blk in range(qt_ref.shape[1]):
            cols = slice(blk * DIFF_BQ, (blk + 1) * DIFF_BQ)
            qt_ref[h, blk, :, :DIFF_BQ] = r1[:, cols]
            qt_ref[h, blk, :, DIFF_BQ:] = r2[:, cols]


def _dk_proj_kernel(x_ref, w_ref, tab_ref, k_ref):
    xb = x_ref[...].astype(BF16)
    dk = jnp.dot(xb, w_ref[...], preferred_element_type=F32)
    for h in range(DIFF_HEADS):
        k_ref[:, h * LANES:(h + 1) * LANES] = _rope_mix(
            dk[:, h * LANES:(h + 1) * LANES], tab_ref, DIFF_ROT // 2).astype(BF16)


def _dv_proj_kernel(x_ref, w_ref, vt_ref):
    xb = x_ref[...].astype(BF16)
    dv = jnp.dot(xb, w_ref[...], preferred_element_type=F32)
    vt = dv.T
    vt_ref[:, 0] = vt.reshape(DIFF_HEADS, DIFF_V, vt.shape[-1]).astype(BF16)


def _gate_kernel(x_ref, w_ref, b_ref, g_ref):
    xb = x_ref[...].astype(BF16)
    z = jnp.dot(xb, w_ref[...], preferred_element_type=F32) + b_ref[...]
    g_ref[...] = (1.0 / (1.0 + jnp.exp(-z))).astype(g_ref.dtype)


def _row_spec(tm, width):
    return pl.BlockSpec((tm, width), lambda i: (i, 0))


def _full_spec(shape):
    nd = len(shape)
    return pl.BlockSpec(shape, lambda i: (0,) * nd, pipeline_mode=pl.Buffered(1))


def _tab_spec(tm, s_tiles):
    return pl.BlockSpec((3, tm, LANES), lambda i: (0, i % s_tiles, 0))


def _vt_out(t, tm, key_chunk):
    per_chunk = key_chunk // tm
    shape = jax.ShapeDtypeStruct((MLA_HEADS, t // key_chunk, MLA_V, key_chunk), BF16)
    spec = pl.BlockSpec((MLA_HEADS, 1, MLA_V, tm), lambda i: (0, i // per_chunk, 0, i % per_chunk))
    return shape, spec


def _q_proj(x, wqa, gqa, wqb, tab, seq):
    t, tm = x.shape[0], PROJ_TM
    blocks = tm // MLA_BQ
    return pl.pallas_call(
        functools.partial(_q_proj_kernel, scale=LOG2_E * (MLA_NOPE + MLA_ROPE) ** -0.5),
        grid=(t // tm,),
        in_specs=[_row_spec(tm, D_MODEL), _full_spec(wqa.shape), _full_spec(gqa.shape),
                  _full_spec(wqb.shape), _tab_spec(tm, seq // tm)],
        out_specs=pl.BlockSpec((MLA_HEADS, blocks, HEAD_PAD, MLA_BQ), lambda i: (0, i, 0, 0)),
        out_shape=jax.ShapeDtypeStruct((MLA_HEADS, t // MLA_BQ, HEAD_PAD, MLA_BQ), BF16),
        compiler_params=_params(("parallel",)),
        name="mla_q_proj",
    )(x, wqa, gqa, wqb, tab)


def _kv_proj(x, wkva, gkva, wkvb, tab, seq):
    t, tm = x.shape[0], VT_TM
    vt_shape, vt_spec = _vt_out(t, tm, MLA_KEY_CHUNK)
    return pl.pallas_call(
        _kv_proj_kernel,
        grid=(t // tm,),
        in_specs=[_row_spec(tm, D_MODEL), _full_spec(wkva.shape), _full_spec(gkva.shape),
                  _full_spec(wkvb.shape), _tab_spec(tm, seq // tm)],
        out_specs=[_row_spec(tm, MLA_HEADS * HEAD_PAD), vt_spec],
        out_shape=[jax.ShapeDtypeStruct((t, MLA_HEADS * HEAD_PAD), BF16), vt_shape],
        compiler_params=_params(("parallel",)),
        name="mla_kv_proj",
    )(x, wkva, gkva, wkvb, tab)


def _dq_proj(x, w, tab, seq):
    t, tm = x.shape[0], PROJ_TM
    blocks = tm // DIFF_BQ
    return pl.pallas_call(
        functools.partial(_dq_proj_kernel, scale=LOG2_E * DIFF_QK ** -0.5),
        grid=(t // tm,),
        in_specs=[_row_spec(tm, D_MODEL), _full_spec(w.shape), _tab_spec(tm, seq // tm)],
        out_specs=pl.BlockSpec((DIFF_HEADS, blocks, LANES, 2 * DIFF_BQ), lambda i: (0, i, 0, 0)),
        out_shape=jax.ShapeDtypeStruct((DIFF_HEADS, t // DIFF_BQ, LANES, 2 * DIFF_BQ), BF16),
        compiler_params=_params(("parallel",)),
        name="diff_q_proj",
    )(x, w, tab)


def _dk_proj(x, w, tab, seq):
    t, tm = x.shape[0], PROJ_TM
    return pl.pallas_call(
        _dk_proj_kernel,
        grid=(t // tm,),
        in_specs=[_row_spec(tm, D_MODEL), _full_spec(w.shape), _tab_spec(tm, seq // tm)],
        out_specs=_row_spec(tm, C_DK),
        out_shape=jax.ShapeDtypeStruct((t, C_DK), BF16),
        compiler_params=_params(("parallel",)),
        name="diff_k_proj",
    )(x, w, tab)


def _dv_proj(x, w):
    t, tm = x.shape[0], PROJ_TM
    vt_shape, vt_spec = _vt_out(t, tm, DIFF_KEY_CHUNK)
    return pl.pallas_call(
        _dv_proj_kernel,
        grid=(t // tm,),
        in_specs=[_row_spec(tm, D_MODEL), _full_spec(w.shape)],
        out_specs=vt_spec,
        out_shape=vt_shape,
        compiler_params=_params(("parallel",)),
        name="diff_v_proj",
    )(x, w)


def _gates(x, w, b):
    t, tm = x.shape[0], PROJ_TM
    n = w.shape[1]
    tn = D_MODEL
    return pl.pallas_call(
        _gate_kernel,
        grid=(n // tn, t // tm),
        in_specs=[pl.BlockSpec((tm, D_MODEL), lambda j, i: (i, 0)),
                  pl.BlockSpec((D_MODEL, tn), lambda j, i: (0, j), pipeline_mode=pl.Buffered(1)),
                  pl.BlockSpec((1, tn), lambda j, i: (0, j))],
        out_specs=pl.BlockSpec((tm, tn), lambda j, i: (i, j)),
        out_shape=jax.ShapeDtypeStruct((t, n), BF16),
        compiler_params=_params(("parallel", "parallel")),
        name="gates",
    )(x, w, b)


def _flash_head(qt_ref, k_ref, vt_ref, s_ref, p_ref, m_ref, l_ref, acc_ref):
    nq, _, nc = qt_ref.shape
    n_chunks, _, key_chunk = vt_ref.shape
    n_steps = n_chunks * nq
    n_tiles = key_chunk // KEY_TILE
    block_bits = nq.bit_length() - 1
    assert nq == 1 << block_bits and n_steps % 2 == 0

    def chunk_and_block(g):
        return lax.shift_right_logical(g, block_bits), lax.bitwise_and(g, nq - 1)

    def sublane_groups(x):
        return x.reshape(KEY_TILE // SUBLANES, SUBLANES, nc)

    def score_tile(g, t):
        chunk, blk = chunk_and_block(g)
        start = pl.multiple_of(chunk * key_chunk + t * KEY_TILE, KEY_TILE)
        return jnp.dot(k_ref[pl.ds(start, KEY_TILE), :], qt_ref[blk], preferred_element_type=F32)

    def add_pv(g_prev, alpha_prev):
        chunk, blk = chunk_and_block(g_prev)
        pv = None
        for h in range(key_chunk // PV_SPAN):
            span = slice(h * PV_SPAN, (h + 1) * PV_SPAN)
            part = jnp.dot(vt_ref[chunk, :, span], p_ref[span, :], preferred_element_type=F32)
            pv = part if h == 0 else pv + part
        acc_ref[blk] = alpha_prev * acc_ref[blk] + pv

    def step(g, carry):
        alpha_prev, s_max = carry
        _, blk = chunk_and_block(g)
        m = m_ref[blk]
        m_new = jnp.maximum(m, s_max)
        alpha = jnp.exp2(m - m_new)
        m_ref[blk] = m_new
        add_pv(jnp.maximum(g - 1, 0), alpha_prev)
        g_next = jnp.minimum(g + 1, n_steps - 1)
        next_max = None
        p_sum = None
        for t in range(n_tiles):
            rows = pl.ds(t * KEY_TILE, KEY_TILE)
            p = jnp.exp2(s_ref[rows, :] - m_new)
            p_ref[rows, :] = p.astype(BF16)
            tile_sum = jnp.sum(sublane_groups(p), axis=0)
            p_sum = tile_sum if t == 0 else p_sum + tile_sum
            s_tile = score_tile(g_next, t)
            s_ref[rows, :] = s_tile
            tile_max = jnp.max(sublane_groups(s_tile), axis=0)
            next_max = tile_max if t == 0 else jnp.maximum(next_max, tile_max)
        l_ref[blk] = alpha * l_ref[blk] + jnp.sum(p_sum, axis=0, keepdims=True)
        return alpha, jnp.max(next_max, axis=0, keepdims=True)

    first_max = None
    for t in range(n_tiles):
        s_tile = score_tile(0, t)
        s_ref[pl.ds(t * KEY_TILE, KEY_TILE), :] = s_tile
        tile_max = jnp.max(sublane_groups(s_tile), axis=0)
        first_max = tile_max if t == 0 else jnp.maximum(first_max, tile_max)
    p_ref[...] = jnp.zeros(p_ref.shape, BF16)
    m_ref[...] = jnp.full(m_ref.shape, -jnp.inf, F32)
    l_ref[...] = jnp.zeros(l_ref.shape, F32)
    acc_ref[...] = jnp.zeros(acc_ref.shape, F32)

    init = (jnp.ones((1, nc), F32), jnp.max(first_max, axis=0, keepdims=True))
    alpha_last, _ = lax.fori_loop(0, n_steps, step, init, unroll=2)
    add_pv(n_steps - 1, alpha_last)


def _mla_flash_kernel(qt_ref, k_ref, vt_ref, o_ref, s_ref, p_ref, m_ref, l_ref, acc_ref):
    _flash_head(qt_ref, k_ref, vt_ref, s_ref, p_ref, m_ref, l_ref, acc_ref)
    nq, _, nc = qt_ref.shape

    def finish(blk, _):
        o = acc_ref[blk] * (1.0 / l_ref[blk])
        o_ref[pl.ds(pl.multiple_of(blk * nc, nc), nc), :] = o.T.astype(BF16)
        return 0

    lax.fori_loop(0, nq, finish, 0)


def _diff_flash_kernel(qt_ref, k_ref, vt_ref, lq_ref, lk_ref, gsub_ref, o_ref,
                       s_ref, p_ref, m_ref, l_ref, acc_ref, *, lam_init):
    _flash_head(qt_ref, k_ref, vt_ref, s_ref, p_ref, m_ref, l_ref, acc_ref)
    nq, _, nc = qt_ref.shape
    bq = nc // 2
    lam_dot = jnp.sum(lq_ref[...] * lk_ref[...], axis=-1, keepdims=True)
    lam_exp = jnp.exp(lam_dot)
    lam = lam_exp[0:1, :] - lam_exp[1:2, :] + lam_init

    def finish(blk, _):
        on = acc_ref[blk] * (1.0 / l_ref[blk])
        a = on[:, :bq] - lam * on[:, bq:]
        ms = jnp.mean(a * a, axis=0, keepdims=True)
        y = (a * lax.rsqrt(ms + RMS_EPS)).T * gsub_ref[...]
        o_ref[pl.ds(pl.multiple_of(blk * bq, bq), bq), :] = (y * (1.0 - lam_init)).astype(BF16)
        return 0

    lax.fori_loop(0, nq, finish, 0)


def _flash_call(kernel, qt, k, vt, extra, batch, seq, dk, dv, nc, key_chunk, name):
    heads = qt.shape[0]
    nq = qt.shape[1] // batch
    n_chunks = seq // key_chunk
    once = pl.Buffered(1)
    small = lambda shape: pl.BlockSpec(shape, lambda b, h: (0, 0))
    return pl.pallas_call(
        kernel,
        grid=(batch, heads),
        in_specs=[pl.BlockSpec((None, nq, dk, nc), lambda b, h: (h, b, 0, 0), pipeline_mode=once),
                  pl.BlockSpec((seq, dk), lambda b, h: (b, h), pipeline_mode=once),
                  pl.BlockSpec((None, n_chunks, dv, key_chunk), lambda b, h: (h, b, 0, 0),
                               pipeline_mode=once)] + [small(e.shape) for e in extra],
        out_specs=pl.BlockSpec((seq, dv), lambda b, h: (b, h)),
        out_shape=jax.ShapeDtypeStruct((batch * seq, heads * dv), BF16),
        scratch_shapes=[pltpu.VMEM((key_chunk, nc), F32), pltpu.VMEM((key_chunk, nc), BF16),
                        pltpu.VMEM((nq, 1, nc), F32), pltpu.VMEM((nq, 1, nc), F32),
                        pltpu.VMEM((nq, dv, nc), F32)],
        compiler_params=_params(("parallel", "parallel")),
        name=name,
    )(qt, k, vt, *extra)


def _mla_flash(qt, k, vt, batch, seq):
    return _flash_call(_mla_flash_kernel, qt, k, vt, (), batch, seq, HEAD_PAD, MLA_V, MLA_BQ,
                       MLA_KEY_CHUNK, "mla_flash")


def _diff_flash(qt, k, vt, lam_q, lam_k, g_sub, batch, seq, lam_init):
    return _flash_call(functools.partial(_diff_flash_kernel, lam_init=lam_init), qt, k, vt,
                       (lam_q, lam_k, g_sub), batch, seq, LANES, DIFF_V, 2 * DIFF_BQ, DIFF_KEY_CHUNK, "diff_flash")


def _merge_kernel(om_ref, od_ref, gm_ref, gd_ref, wm_ref, wd_ref, o_ref):
    a = jnp.dot(om_ref[...], wm_ref[...], preferred_element_type=F32)
    b = jnp.dot(od_ref[...], wd_ref[...], preferred_element_type=F32)
    o_ref[...] = (gm_ref[...] * a + gd_ref[...] * b).astype(BF16)


def _outproj_ln_kernel(x_ref, m_ref, w_ref, g_ref, b_ref, o_ref):
    h = jnp.dot(m_ref[...], w_ref[...], preferred_element_type=F32)
    o_ref[...] = _layernorm_rows(DN_ALPHA * x_ref[...] + h, g_ref[...], b_ref[...])


def _ffn_ln_kernel(x_ref, w1_ref, w2_ref, g_ref, b_ref, o_ref, xb_ref, acc_ref):
    j = pl.program_id(1)

    @pl.when(j == 0)
    def _():
        xb_ref[...] = x_ref[...].astype(BF16)
        acc_ref[...] = jnp.zeros_like(acc_ref)

    h = jnp.maximum(jnp.dot(xb_ref[...], w1_ref[...], preferred_element_type=F32), 0.0)
    acc_ref[...] += jnp.dot((h * h).astype(BF16), w2_ref[...], preferred_element_type=F32)

    @pl.when(j == pl.num_programs(1) - 1)
    def _():
        o_ref[...] = _layernorm_rows(DN_ALPHA * x_ref[...] + acc_ref[...], g_ref[...], b_ref[...])


def _merge(o_mla, o_diff, gates, w_br_mla, w_br_diff):
    t, tm = o_mla.shape[0], 512
    return pl.pallas_call(
        _merge_kernel,
        grid=(t // tm,),
        in_specs=[_row_spec(tm, o_mla.shape[1]), _row_spec(tm, o_diff.shape[1]),
                  pl.BlockSpec((tm, D_MODEL), lambda i: (i, 0)),
                  pl.BlockSpec((tm, D_MODEL), lambda i: (i, 1)),
                  _full_spec(w_br_mla.shape), _full_spec(w_br_diff.shape)],
        out_specs=_row_spec(tm, D_MODEL),
        out_shape=jax.ShapeDtypeStruct((t, D_MODEL), BF16),
        compiler_params=_params(("parallel",)),
        name="branch_merge",
    )(o_mla, o_diff, gates, gates, w_br_mla, w_br_diff)


def _outproj_ln(x, merged, w_out, g, b):
    t, tm = x.shape[0], 512
    return pl.pallas_call(
        _outproj_ln_kernel,
        grid=(t // tm,),
        in_specs=[_row_spec(tm, D_MODEL), _row_spec(tm, D_MODEL), _full_spec(w_out.shape),
                  _full_spec(g.shape), _full_spec(b.shape)],
        out_specs=_row_spec(tm, D_MODEL),
        out_shape=jax.ShapeDtypeStruct((t, D_MODEL), F32),
        compiler_params=_params(("parallel",)),
        name="outproj_ln",
    )(x, merged, w_out, g, b)


def _ffn_ln(x, w1, w2, g, b):
    t, tm, tf = x.shape[0], 512, 1024
    return pl.pallas_call(
        _ffn_ln_kernel,
        grid=(t // tm, D_FF // tf),
        in_specs=[pl.BlockSpec((tm, D_MODEL), lambda i, j: (i, 0)),
                  pl.BlockSpec((D_MODEL, tf), lambda i, j: (0, j)),
                  pl.BlockSpec((tf, D_MODEL), lambda i, j: (j, 0)),
                  pl.BlockSpec((1, D_MODEL), lambda i, j: (0, 0)),
                  pl.BlockSpec((1, D_MODEL), lambda i, j: (0, 0))],
        out_specs=pl.BlockSpec((tm, D_MODEL), lambda i, j: (i, 0)),
        out_shape=jax.ShapeDtypeStruct((t, D_MODEL), F32),
        scratch_shapes=[pltpu.VMEM((tm, D_MODEL), BF16), pltpu.VMEM((tm, D_MODEL), F32)],
        compiler_params=_params(("parallel", "arbitrary")),
        name="ffn_ln",
    )(x, w1, w2, g, b)


def _rope_tables(seq, rot_dim, group):
    half = rot_dim // 2
    inv_freq = ROPE_THETA ** (-jnp.arange(0, rot_dim, 2, dtype=F32) / rot_dim)
    ang = jnp.arange(seq, dtype=F32)[:, None] * inv_freq[None, :]
    cos, sin = jnp.cos(ang), jnp.sin(ang)
    zeros = lambda n: jnp.zeros((seq, n), F32)
    c = jnp.concatenate([cos, cos, jnp.ones((seq, group - rot_dim), F32)], axis=1)
    s_fwd = jnp.concatenate([-sin, zeros(group - half)], axis=1)
    s_bwd = jnp.concatenate([zeros(half), sin, zeros(group - rot_dim)], axis=1)
    return jnp.stack([c, s_fwd, s_bwd])


def _prep_weights(w_in, b_gate, g_qa, w_qb, g_kva, w_kvb, g_sub, w_br_mla, w_br_diff, w_out,
                  ln1_g, ln1_b, w_ff1, w_ff2, ln2_g, ln2_b, l):
    c0 = C_QA
    c1 = c0 + C_KVA
    c2 = c1 + C_DQ
    c3 = c2 + C_DK
    c4 = c3 + C_DV
    wi = w_in[l]
    qb = w_qb[l].reshape(MLA_Q_LORA, MLA_HEADS, MLA_NOPE + MLA_ROPE)
    qb_nope = qb[:, :, :MLA_NOPE].reshape(MLA_Q_LORA, MLA_HEADS * MLA_NOPE)
    qb_rope = jnp.pad(qb[:, :, MLA_NOPE:], ((0, 0), (0, 0), (0, LANES - MLA_ROPE)))
    qb_rope = qb_rope.reshape(MLA_Q_LORA, MLA_HEADS * LANES)
    kvb = w_kvb[l].reshape(MLA_KV_LORA, MLA_HEADS, MLA_NOPE + MLA_V)
    kvb_k = kvb[:, :, :MLA_NOPE].reshape(MLA_KV_LORA, MLA_HEADS * MLA_NOPE)
    kvb_v = kvb[:, :, MLA_NOPE:].reshape(MLA_KV_LORA, MLA_HEADS * MLA_V)
    row = lambda v: v[l].reshape(1, -1)
    return dict(
        wqa=wi[:, :c0].astype(BF16),
        wkva=jnp.pad(wi[:, c0:c1], ((0, 0), (0, LANES - MLA_ROPE))).astype(BF16),
        wdq=wi[:, c1:c2].astype(BF16),
        wdk=wi[:, c2:c3].astype(BF16),
        wdv=wi[:, c3:c4].astype(BF16),
        wgate=wi[:, c4:].astype(BF16),
        b_gate=row(b_gate),
        g_qa=row(g_qa),
        wqb=jnp.concatenate([qb_nope, qb_rope], axis=1).astype(BF16),
        g_kva=row(g_kva),
        wkvb=jnp.concatenate([kvb_k, kvb_v], axis=1).astype(BF16),
        g_sub=row(g_sub),
        w_br_mla=w_br_mla[l].astype(BF16),
        w_br_diff=w_br_diff[l].astype(BF16),
        w_out=w_out[l].astype(BF16),
        ln1_g=row(ln1_g), ln1_b=row(ln1_b),
        w_ff1=w_ff1[l].astype(BF16), w_ff2=w_ff2[l].astype(BF16),
        ln2_g=row(ln2_g), ln2_b=row(ln2_b),
    )


def _layer(x3, w, tab_mla, tab_diff, lam_q, lam_k, lam_init):
    batch, seq, _ = x3.shape
    x = x3.reshape(batch * seq, D_MODEL)

    qt = _q_proj(x, w["wqa"], w["g_qa"], w["wqb"], tab_mla, seq)
    k, vt = _kv_proj(x, w["wkva"], w["g_kva"], w["wkvb"], tab_mla, seq)
    o_mla = _mla_flash(qt, k, vt, batch, seq)

    dqt = _dq_proj(x, w["wdq"], tab_diff, seq)
    dk = _dk_proj(x, w["wdk"], tab_diff, seq)
    dvt = _dv_proj(x, w["wdv"])
    o_diff = _diff_flash(dqt, dk, dvt, lam_q, lam_k, w["g_sub"], batch, seq, lam_init)

    gates = _gates(x, w["wgate"], w["b_gate"])
    merged = _merge(o_mla, o_diff, gates, w["w_br_mla"], w["w_br_diff"])
    x1 = _outproj_ln(x, merged, w["w_out"], w["ln1_g"], w["ln1_b"])
    y = _ffn_ln(x1, w["w_ff1"], w["w_ff2"], w["ln2_g"], w["ln2_b"])
    return y.reshape(batch, seq, D_MODEL)


def kernel(x_prompt, x_sample, w_in, b_gate, g_qa, w_qb, g_kva, w_kvb, lam_q, lam_k, g_sub,
           w_br_mla, w_br_diff, w_out, ln1_g, ln1_b, w_ff1, w_ff2, ln2_g, ln2_b):
    outs = [x_prompt, x_sample]
    max_seq = max(x3.shape[1] for x3 in outs)
    tab_mla = jnp.pad(_rope_tables(max_seq, MLA_ROPE, MLA_ROPE), ((0, 0), (0, 0), (0, LANES - MLA_ROPE)))
    tab_diff = jnp.tile(_rope_tables(max_seq, DIFF_ROT, DIFF_QK), (1, 1, LANES // DIFF_QK))
    for l in range(DEPTH):
        lam_init = 0.8 - 0.6 * math.exp(-0.3 * l)
        w = _prep_weights(w_in, b_gate, g_qa, w_qb, g_kva, w_kvb, g_sub, w_br_mla, w_br_diff, w_out,
                          ln1_g, ln1_b, w_ff1, w_ff2, ln2_g, ln2_b, l)
        outs = [_layer(x3, w, tab_mla, tab_diff, lam_q[l], lam_k[l], lam_init) for x3 in outs]
    return tuple(outs)
```

```python
import functools
import math

import jax
import jax.numpy as jnp
from jax import lax
from jax.experimental import pallas as pl
from jax.experimental.pallas import tpu as pltpu

D_MODEL = 2048
DEPTH = 1
MLA_HEADS = 8
MLA_Q_LORA = 768
MLA_KV_LORA = 512
MLA_NOPE = 128
MLA_ROPE = 64
MLA_V = 128
DIFF_HEADS = 8
DIFF_QK = 64
DIFF_V = 2 * DIFF_QK
DIFF_ROT = DIFF_QK // 4
D_FF = 4 * D_MODEL
ROPE_THETA = 500000.0
LN_EPS = 1e-5
RMS_EPS = 1e-6
DN_ALPHA = (2.0 * DEPTH) ** 0.25
LOG2_E = math.log2(math.e)

C_QA = MLA_Q_LORA
C_KVA = MLA_KV_LORA + MLA_ROPE
C_DQ = DIFF_HEADS * 2 * DIFF_QK
C_DK = DIFF_HEADS * 2 * DIFF_QK
C_DV = DIFF_HEADS * DIFF_V

LANES = 128
HEAD_PAD = 2 * LANES
MLA_KEY_CHUNK = 2048
DIFF_KEY_CHUNK = 2048
KEY_TILE = 1024
PV_SPAN = 1024
SUBLANES = 8
VT_TM = 512
PROJ_TM = 1024
MLA_BQ = 512
DIFF_BQ = 256
VMEM_LIMIT = 56 * 1024 * 1024

F32 = jnp.float32
BF16 = jnp.bfloat16


def _params(sem):
    return pltpu.CompilerParams(dimension_semantics=sem, vmem_limit_bytes=VMEM_LIMIT)


def _rope_mix(x, tab_ref, shift):
    n = x.shape[-1]
    fwd = pltpu.roll(x, n - shift, 1)
    bwd = pltpu.roll(x, shift, 1)
    return x * tab_ref[0] + fwd * tab_ref[1] + bwd * tab_ref[2]


def _rmsnorm_rows(x, g):
    ms = jnp.mean(x * x, axis=-1, keepdims=True)
    return x * lax.rsqrt(ms + RMS_EPS) * g


def _layernorm_rows(z, g, b):
    mu = jnp.mean(z, axis=-1, keepdims=True)
    zc = z - mu
    var = jnp.mean(zc * zc, axis=-1, keepdims=True)
    return zc * lax.rsqrt(var + LN_EPS) * g + b


def _q_proj_kernel(x_ref, wqa_ref, gqa_ref, wqb_ref, tab_ref, qt_ref, *, scale):
    xb = x_ref[...].astype(BF16)
    qa = jnp.dot(xb, wqa_ref[...], preferred_element_type=F32)
    qn = _rmsnorm_rows(qa, gqa_ref[...]).astype(BF16)
    q = jnp.dot(qn, wqb_ref[...], preferred_element_type=F32)
    nope_w = MLA_HEADS * MLA_NOPE
    for h in range(MLA_HEADS):
        nope = q[:, h * MLA_NOPE:(h + 1) * MLA_NOPE]
        rope = _rope_mix(q[:, nope_w + h * LANES: nope_w + (h + 1) * LANES], tab_ref, MLA_ROPE // 2)
        nope_t = (nope * scale).T.astype(BF16)
        rope_t = (rope * scale).T.astype(BF16)
        for blk in range(qt_ref.shape[1]):
            cols = slice(blk * MLA_BQ, (blk + 1) * MLA_BQ)
            qt_ref[h, blk, :LANES, :] = nope_t[:, cols]
            qt_ref[h, blk, LANES:, :] = rope_t[:, cols]


def _kv_proj_kernel(x_ref, wkva_ref, gkva_ref, wkvb_ref, tab_ref, k_ref, vt_ref):
    xb = x_ref[...].astype(BF16)
    kva = jnp.dot(xb, wkva_ref[...], preferred_element_type=F32)
    ckv = _rmsnorm_rows(kva[:, :MLA_KV_LORA], gkva_ref[...]).astype(BF16)
    krope = _rope_mix(kva[:, MLA_KV_LORA:], tab_ref, MLA_ROPE // 2).astype(BF16)
    kv = jnp.dot(ckv, wkvb_ref[...], preferred_element_type=F32)
    nope_w = MLA_HEADS * MLA_NOPE
    for h in range(MLA_HEADS):
        k_ref[:, h * HEAD_PAD: h * HEAD_PAD + LANES] = kv[:, h * MLA_NOPE:(h + 1) * MLA_NOPE].astype(BF16)
        k_ref[:, h * HEAD_PAD + LANES:(h + 1) * HEAD_PAD] = krope
    vt = kv[:, nope_w:].T
    vt_ref[:, 0] = vt.reshape(MLA_HEADS, MLA_V, vt.shape[-1]).astype(BF16)


def _dq_proj_kernel(x_ref, w_ref, tab_ref, qt_ref, *, scale):
    xb = x_ref[...].astype(BF16)
    dq = jnp.dot(xb, w_ref[...], preferred_element_type=F32)
    first_map = lax.broadcasted_iota(jnp.int32, (dq.shape[0], LANES), 1) < DIFF_QK
    for h in range(DIFF_HEADS):
        r = _rope_mix(dq[:, h * LANES:(h + 1) * LANES], tab_ref, DIFF_ROT // 2) * scale
        r1 = jnp.where(first_map, r, 0.0).T.astype(BF16)
        r2 = jnp.where(first_map, 0.0, r).T.astype(BF16)
        for blk in range(qt_ref.shape[1]):
            cols = slice(blk * DIFF_BQ, (blk + 1) * DIFF_BQ)
            qt_ref[h, blk, :, :DIFF_BQ] = r1[:, cols]
            qt_ref[h, blk, :, DIFF_BQ:] = r2[:, cols]


def _dk_proj_kernel(x_ref, w_ref, tab_ref, k_ref):
    xb = x_ref[...].astype(BF16)
    dk = jnp.dot(xb, w_ref[...], preferred_element_type=F32)
    for h in range(DIFF_HEADS):
        k_ref[:, h * LANES:(h + 1) * LANES] = _rope_mix(
            dk[:, h * LANES:(h + 1) * LANES], tab_ref, DIFF_ROT // 2).astype(BF16)


def _dv_proj_kernel(x_ref, w_ref, vt_ref):
    xb = x_ref[...].astype(BF16)
    dv = jnp.dot(xb, w_ref[...], preferred_element_type=F32)
    vt = dv.T
    vt_ref[:, 0] = vt.reshape(DIFF_HEADS, DIFF_V, vt.shape[-1]).astype(BF16)


def _gate_kernel(x_ref, w_ref, b_ref, g_ref):
    xb = x_ref[...].astype(BF16)
    z = jnp.dot(xb, w_ref[...], preferred_element_type=F32) + b_ref[...]
    g_ref[...] = (1.0 / (1.0 + jnp.exp(-z))).astype(g_ref.dtype)


def _row_spec(tm, width):
    return pl.BlockSpec((tm, width), lambda i: (i, 0))


def _full_spec(shape):
    nd = len(shape)
    return pl.BlockSpec(shape, lambda i: (0,) * nd, pipeline_mode=pl.Buffered(1))


def _tab_spec(tm, s_tiles):
    return pl.BlockSpec((3, tm, LANES), lambda i: (0, i % s_tiles, 0))


def _vt_out(t, tm, key_chunk):
    per_chunk = key_chunk // tm
    shape = jax.ShapeDtypeStruct((MLA_HEADS, t // key_chunk, MLA_V, key_chunk), BF16)
    spec = pl.BlockSpec((MLA_HEADS, 1, MLA_V, tm), lambda i: (0, i // per_chunk, 0, i % per_chunk))
    return shape, spec


def _q_proj(x, wqa, gqa, wqb, tab, seq):
    t, tm = x.shape[0], PROJ_TM
    blocks = tm // MLA_BQ
    return pl.pallas_call(
        functools.partial(_q_proj_kernel, scale=LOG2_E * (MLA_NOPE + MLA_ROPE) ** -0.5),
        grid=(t // tm,),
        in_specs=[_row_spec(tm, D_MODEL), _full_spec(wqa.shape), _full_spec(gqa.shape),
                  _full_spec(wqb.shape), _tab_spec(tm, seq // tm)],
        out_specs=pl.BlockSpec((MLA_HEADS, blocks, HEAD_PAD, MLA_BQ), lambda i: (0, i, 0, 0)),
        out_shape=jax.ShapeDtypeStruct((MLA_HEADS, t // MLA_BQ, HEAD_PAD, MLA_BQ), BF16),
        compiler_params=_params(("parallel",)),
        name="mla_q_proj",
    )(x, wqa, gqa, wqb, tab)


def _kv_proj(x, wkva, gkva, wkvb, tab, seq):
    t, tm = x.shape[0], VT_TM
    vt_shape, vt_spec = _vt_out(t, tm, MLA_KEY_CHUNK)
    return pl.pallas_call(
        _kv_proj_kernel,
        grid=(t // tm,),
        in_specs=[_row_spec(tm, D_MODEL), _full_spec(wkva.shape), _full_spec(gkva.shape),
                  _full_spec(wkvb.shape), _tab_spec(tm, seq // tm)],
        out_specs=[_row_spec(tm, MLA_HEADS * HEAD_PAD), vt_spec],
        out_shape=[jax.ShapeDtypeStruct((t, MLA_HEADS * HEAD_PAD), BF16), vt_shape],
        compiler_params=_params(("parallel",)),
        name="mla_kv_proj",
    )(x, wkva, gkva, wkvb, tab)


def _dq_proj(x, w, tab, seq):
    t, tm = x.shape[0], PROJ_TM
    blocks = tm // DIFF_BQ
    return pl.pallas_call(
        functools.partial(_dq_proj_kernel, scale=LOG2_E * DIFF_QK ** -0.5),
        grid=(t // tm,),
        in_specs=[_row_spec(tm, D_MODEL), _full_spec(w.shape), _tab_spec(tm, seq // tm)],
        out_specs=pl.BlockSpec((DIFF_HEADS, blocks, LANES, 2 * DIFF_BQ), lambda i: (0, i, 0, 0)),
        out_shape=jax.ShapeDtypeStruct((DIFF_HEADS, t // DIFF_BQ, LANES, 2 * DIFF_BQ), BF16),
        compiler_params=_params(("parallel",)),
        name="diff_q_proj",
    )(x, w, tab)


def _dk_proj(x, w, tab, seq):
    t, tm = x.shape[0], PROJ_TM
    return pl.pallas_call(
        _dk_proj_kernel,
        grid=(t // tm,),
        in_specs=[_row_spec(tm, D_MODEL), _full_spec(w.shape), _tab_spec(tm, seq // tm)],
        out_specs=_row_spec(tm, C_DK),
        out_shape=jax.ShapeDtypeStruct((t, C_DK), BF16),
        compiler_params=_params(("parallel",)),
        name="diff_k_proj",
    )(x, w, tab)


def _dv_proj(x, w):
    t, tm = x.shape[0], PROJ_TM
    vt_shape, vt_spec = _vt_out(t, tm, DIFF_KEY_CHUNK)
    return pl.pallas_call(
        _dv_proj_kernel,
        grid=(t // tm,),
        in_specs=[_row_spec(tm, D_MODEL), _full_spec(w.shape)],
        out_specs=vt_spec,
        out_shape=vt_shape,
        compiler_params=_params(("parallel",)),
        name="diff_v_proj",
    )(x, w)


def _gates(x, w, b):
    t, tm = x.shape[0], PROJ_TM
    n = w.shape[1]
    tn = D_MODEL
    return pl.pallas_call(
        _gate_kernel,
        grid=(n // tn, t // tm),
        in_specs=[pl.BlockSpec((tm, D_MODEL), lambda j, i: (i, 0)),
                  pl.BlockSpec((D_MODEL, tn), lambda j, i: (0, j), pipeline_mode=pl.Buffered(1)),
                  pl.BlockSpec((1, tn), lambda j, i: (0, j))],
        out_specs=pl.BlockSpec((tm, tn), lambda j, i: (i, j)),
        out_shape=jax.ShapeDtypeStruct((t, n), BF16),
        compiler_params=_params(("parallel", "parallel")),
        name="gates",
    )(x, w, b)


def _flash_head(qt_ref, k_ref, vt_ref, s_ref, p_ref, m_ref, l_ref, acc_ref):
    nq, _, nc = qt_ref.shape
    n_chunks, _, key_chunk = vt_ref.shape
    n_steps = n_chunks * nq
    n_tiles = key_chunk // KEY_TILE
    block_bits = nq.bit_length() - 1
    assert nq == 1 << block_bits and n_steps % 2 == 0

    def chunk_and_block(g):
        return lax.shift_right_logical(g, block_bits), lax.bitwise_and(g, nq - 1)

    def sublane_groups(x):
        return x.reshape(KEY_TILE // SUBLANES, SUBLANES, nc)

    def score_tile(g, t):
        chunk, blk = chunk_and_block(g)
        start = pl.multiple_of(chunk * key_chunk + t * KEY_TILE, KEY_TILE)
        return jnp.dot(k_ref[pl.ds(start, KEY_TILE), :], qt_ref[blk], preferred_element_type=F32)

    def add_pv(g_prev, alpha_prev):
        chunk, blk = chunk_and_block(g_prev)
        pv = None
        for h in range(key_chunk // PV_SPAN):
            span = slice(h * PV_SPAN, (h + 1) * PV_SPAN)
            part = jnp.dot(vt_ref[chunk, :, span], p_ref[span, :], preferred_element_type=F32)
            pv = part if h == 0 else pv + part
        acc_ref[blk] = alpha_prev * acc_ref[blk] + pv

    def step(g, carry):
        alpha_prev, s_max = carry
        _, blk = chunk_and_block(g)
        m = m_ref[blk]
        m_new = jnp.maximum(m, s_max)
        alpha = jnp.exp2(m - m_new)
        m_ref[blk] = m_new
        add_pv(jnp.maximum(g - 1, 0), alpha_prev)
        g_next = jnp.minimum(g + 1, n_steps - 1)
        next_max = None
        p_sum = None
        for t in range(n_tiles):
            rows = pl.ds(t * KEY_TILE, KEY_TILE)
            p = jnp.exp2(s_ref[rows, :] - m_new)
            p_ref[rows, :] = p.astype(BF16)
            tile_sum = jnp.sum(sublane_groups(p), axis=0)
            p_sum = tile_sum if t == 0 else p_sum + tile_sum
            s_tile = score_tile(g_next, t)
            s_ref[rows, :] = s_tile
            tile_max = jnp.max(sublane_groups(s_tile), axis=0)
            next_max = tile_max if t == 0 else jnp.maximum(next_max, tile_max)
        l_ref[blk] = alpha * l_ref[blk] + jnp.sum(p_sum, axis=0, keepdims=True)
        return alpha, jnp.max(next_max, axis=0, keepdims=True)

    first_max = None
    for t in range(n_tiles):
        s_tile = score_tile(0, t)
        s_ref[pl.ds(t * KEY_TILE, KEY_TILE), :] = s_tile
        tile_max = jnp.max(sublane_groups(s_tile), axis=0)
        first_max = tile_max if t == 0 else jnp.maximum(first_max, tile_max)
    p_ref[...] = jnp.zeros(p_ref.shape, BF16)
    m_ref[...] = jnp.full(m_ref.shape, -jnp.inf, F32)
    l_ref[...] = jnp.zeros(l_ref.shape, F32)
    acc_ref[...] = jnp.zeros(acc_ref.shape, F32)

    init = (jnp.ones((1, nc), F32), jnp.max(first_max, axis=0, keepdims=True))
    alpha_last, _ = lax.fori_loop(0, n_steps, step, init, unroll=2)
    add_pv(n_steps - 1, alpha_last)


def _mla_flash_kernel(qt_ref, k_ref, vt_ref, o_ref, s_ref, p_ref, m_ref, l_ref, acc_ref):
    _flash_head(qt_ref, k_ref, vt_ref, s_ref, p_ref, m_ref, l_ref, acc_ref)
    nq, _, nc = qt_ref.shape

    def finish(blk, _):
        o = acc_ref[blk] * (1.0 / l_ref[blk])
        o_ref[pl.ds(pl.multiple_of(blk * nc, nc), nc), :] = o.T.astype(BF16)
        return 0

    lax.fori_loop(0, nq, finish, 0)


def _diff_flash_kernel(qt_ref, k_ref, vt_ref, lq_ref, lk_ref, gsub_ref, o_ref,
                       s_ref, p_ref, m_ref, l_ref, acc_ref, *, lam_init):
    _flash_head(qt_ref, k_ref, vt_ref, s_ref, p_ref, m_ref, l_ref, acc_ref)
    nq, _, nc = qt_ref.shape
    bq = nc // 2
    lam_dot = jnp.sum(lq_ref[...] * lk_ref[...], axis=-1, keepdims=True)
    lam_exp = jnp.exp(lam_dot)
    lam = lam_exp[0:1, :] - lam_exp[1:2, :] + lam_init

    def finish(blk, _):
        on = acc_ref[blk] * (1.0 / l_ref[blk])
        a = on[:, :bq] - lam * on[:, bq:]
        ms = jnp.mean(a * a, axis=0, keepdims=True)
        y = (a * lax.rsqrt(ms + RMS_EPS)).T * gsub_ref[...]
        o_ref[pl.ds(pl.multiple_of(blk * bq, bq), bq), :] = (y * (1.0 - lam_init)).astype(BF16)
        return 0

    lax.fori_loop(0, nq, finish, 0)


def _flash_call(kernel, qt, k, vt, extra, batch, seq, dk, dv, nc, key_chunk, name):
    heads = qt.shape[0]
    nq = qt.shape[1] // batch
    n_chunks = seq // key_chunk
    once = pl.Buffered(1)
    small = lambda shape: pl.BlockSpec(shape, lambda b, h: (0, 0))
    return pl.pallas_call(
        kernel,
        grid=(batch, heads),
        in_specs=[pl.BlockSpec((None, nq, dk, nc), lambda b, h: (h, b, 0, 0), pipeline_mode=once),
                  pl.BlockSpec((seq, dk), lambda b, h: (b, h), pipeline_mode=once),
                  pl.BlockSpec((None, n_chunks, dv, key_chunk), lambda b, h: (h, b, 0, 0),
                               pipeline_mode=once)] + [small(e.shape) for e in extra],
        out_specs=pl.BlockSpec((seq, dv), lambda b, h: (b, h)),
        out_shape=jax.ShapeDtypeStruct((batch * seq, heads * dv), BF16),
        scratch_shapes=[pltpu.VMEM((key_chunk, nc), F32), pltpu.VMEM((key_chunk, nc), BF16),
                        pltpu.VMEM((nq, 1, nc), F32), pltpu.VMEM((nq, 1, nc), F32),
                        pltpu.VMEM((nq, dv, nc), F32)],
        compiler_params=_params(("parallel", "parallel")),
        name=name,
    )(qt, k, vt, *extra)


def _mla_flash(qt, k, vt, batch, seq):
    return _flash_call(_mla_flash_kernel, qt, k, vt, (), batch, seq, HEAD_PAD, MLA_V, MLA_BQ,
                       MLA_KEY_CHUNK, "mla_flash")


def _diff_flash(qt, k, vt, lam_q, lam_k, g_sub, batch, seq, lam_init):
    return _flash_call(functools.partial(_diff_flash_kernel, lam_init=lam_init), qt, k, vt,
                       (lam_q, lam_k, g_sub), batch, seq, LANES, DIFF_V, 2 * DIFF_BQ, DIFF_KEY_CHUNK, "diff_flash")


def _merge_kernel(om_ref, od_ref, gm_ref, gd_ref, wm_ref, wd_ref, o_ref):
    a = jnp.dot(om_ref[...], wm_ref[...], preferred_element_type=F32)
    b = jnp.dot(od_ref[...], wd_ref[...], preferred_element_type=F32)
    o_ref[...] = (gm_ref[...] * a + gd_ref[...] * b).astype(BF16)


def _outproj_ln_kernel(x_ref, m_ref, w_ref, g_ref, b_ref, o_ref):
    h = jnp.dot(m_ref[...], w_ref[...], preferred_element_type=F32)
    o_ref[...] = _layernorm_rows(DN_ALPHA * x_ref[...] + h, g_ref[...], b_ref[...])


def _ffn_ln_kernel(x_ref, w1_ref, w2_ref, g_ref, b_ref, o_ref, xb_ref, acc_ref):
    j = pl.program_id(1)

    @pl.when(j == 0)
    def _():
        xb_ref[...] = x_ref[...].astype(BF16)
        acc_ref[...] = jnp.zeros_like(acc_ref)

    h = jnp.maximum(jnp.dot(xb_ref[...], w1_ref[...], preferred_element_type=F32), 0.0)
    acc_ref[...] += jnp.dot((h * h).astype(BF16), w2_ref[...], preferred_element_type=F32)

    @pl.when(j == pl.num_programs(1) - 1)
    def _():
        o_ref[...] = _layernorm_rows(DN_ALPHA * x_ref[...] + acc_ref[...], g_ref[...], b_ref[...])


def _merge(o_mla, o_diff, gates, w_br_mla, w_br_diff):
    t, tm = o_mla.shape[0], 512
    return pl.pallas_call(
        _merge_kernel,
        grid=(t // tm,),
        in_specs=[_row_spec(tm, o_mla.shape[1]), _row_spec(tm, o_diff.shape[1]),
                  pl.BlockSpec((tm, D_MODEL), lambda i: (i, 0)),
                  pl.BlockSpec((tm, D_MODEL), lambda i: (i, 1)),
                  _full_spec(w_br_mla.shape), _full_spec(w_br_diff.shape)],
        out_specs=_row_spec(tm, D_MODEL),
        out_shape=jax.ShapeDtypeStruct((t, D_MODEL), BF16),
        compiler_params=_params(("parallel",)),
        name="branch_merge",
    )(o_mla, o_diff, gates, gates, w_br_mla, w_br_diff)


def _outproj_ln(x, merged, w_out, g, b):
    t, tm = x.shape[0], 512
    return pl.pallas_call(
        _outproj_ln_kernel,
        grid=(t // tm,),
        in_specs=[_row_spec(tm, D_MODEL), _row_spec(tm, D_MODEL), _full_spec(w_out.shape),
                  _full_spec(g.shape), _full_spec(b.shape)],
        out_specs=_row_spec(tm, D_MODEL),
        out_shape=jax.ShapeDtypeStruct((t, D_MODEL), F32),
        compiler_params=_params(("parallel",)),
        name="outproj_ln",
    )(x, merged, w_out, g, b)


def _ffn_ln(x, w1, w2, g, b):
    t, tm, tf = x.shape[0], 512, 1024
    return pl.pallas_call(
        _ffn_ln_kernel,
        grid=(t // tm, D_FF // tf),
        in_specs=[pl.BlockSpec((tm, D_MODEL), lambda i, j: (i, 0)),
                  pl.BlockSpec((D_MODEL, tf), lambda i, j: (0, j)),
                  pl.BlockSpec((tf, D_MODEL), lambda i, j: (j, 0)),
                  pl.BlockSpec((1, D_MODEL), lambda i, j: (0, 0)),
                  pl.BlockSpec((1, D_MODEL), lambda i, j: (0, 0))],
        out_specs=pl.BlockSpec((tm, D_MODEL), lambda i, j: (i, 0)),
        out_shape=jax.ShapeDtypeStruct((t, D_MODEL), F32),
        scratch_shapes=[pltpu.VMEM((tm, D_MODEL), BF16), pltpu.VMEM((tm, D_MODEL), F32)],
        compiler_params=_params(("parallel", "arbitrary")),
        name="ffn_ln",
    )(x, w1, w2, g, b)


def _rope_tables(seq, rot_dim, group):
    half = rot_dim // 2
    inv_freq = ROPE_THETA ** (-jnp.arange(0, rot_dim, 2, dtype=F32) / rot_dim)
    ang = jnp.arange(seq, dtype=F32)[:, None] * inv_freq[None, :]
    cos, sin = jnp.cos(ang), jnp.sin(ang)
    zeros = lambda n: jnp.zeros((seq, n), F32)
    c = jnp.concatenate([cos, cos, jnp.ones((seq, group - rot_dim), F32)], axis=1)
    s_fwd = jnp.concatenate([-sin, zeros(group - half)], axis=1)
    s_bwd = jnp.concatenate([zeros(half), sin, zeros(group - rot_dim)], axis=1)
    return jnp.stack([c, s_fwd, s_bwd])


def _prep_weights(w_in, b_gate, g_qa, w_qb, g_kva, w_kvb, g_sub, w_br_mla, w_br_diff, w_out,
                  ln1_g, ln1_b, w_ff1, w_ff2, ln2_g, ln2_b, l):
    c0 = C_QA
    c1 = c0 + C_KVA
    c2 = c1 + C_DQ
    c3 = c2 + C_DK
    c4 = c3 + C_DV
    wi = w_in[l]
    qb = w_qb[l].reshape(MLA_Q_LORA, MLA_HEADS, MLA_NOPE + MLA_ROPE)
    qb_nope = qb[:, :, :MLA_NOPE].reshape(MLA_Q_LORA, MLA_HEADS * MLA_NOPE)
    qb_rope = jnp.pad(qb[:, :, MLA_NOPE:], ((0, 0), (0, 0), (0, LANES - MLA_ROPE)))
    qb_rope = qb_rope.reshape(MLA_Q_LORA, MLA_HEADS * LANES)
    kvb = w_kvb[l].reshape(MLA_KV_LORA, MLA_HEADS, MLA_NOPE + MLA_V)
    kvb_k = kvb[:, :, :MLA_NOPE].reshape(MLA_KV_LORA, MLA_HEADS * MLA_NOPE)
    kvb_v = kvb[:, :, MLA_NOPE:].reshape(MLA_KV_LORA, MLA_HEADS * MLA_V)
    row = lambda v: v[l].reshape(1, -1)
    return dict(
        wqa=wi[:, :c0].astype(BF16),
        wkva=jnp.pad(wi[:, c0:c1], ((0, 0), (0, LANES - MLA_ROPE))).astype(BF16),
        wdq=wi[:, c1:c2].astype(BF16),
        wdk=wi[:, c2:c3].astype(BF16),
        wdv=wi[:, c3:c4].astype(BF16),
        wgate=wi[:, c4:].astype(BF16),
        b_gate=row(b_gate),
        g_qa=row(g_qa),
        wqb=jnp.concatenate([qb_nope, qb_rope], axis=1).astype(BF16),
        g_kva=row(g_kva),
        wkvb=jnp.concatenate([kvb_k, kvb_v], axis=1).astype(BF16),
        g_sub=row(g_sub),
        w_br_mla=w_br_mla[l].astype(BF16),
        w_br_diff=w_br_diff[l].astype(BF16),
        w_out=w_out[l].astype(BF16),
        ln1_g=row(ln1_g), ln1_b=row(ln1_b),
        w_ff1=w_ff1[l].astype(BF16), w_ff2=w_ff2[l].astype(BF16),
        ln2_g=row(ln2_g), ln2_b=row(ln2_b),
    )


def _layer(x3, w, tab_mla, tab_diff, lam_q, lam_k, lam_init):
    batch, seq, _ = x3.shape
    x = x3.reshape(batch * seq, D_MODEL)

    qt = _q_proj(x, w["wqa"], w["g_qa"], w["wqb"], tab_mla, seq)
    k, vt = _kv_proj(x, w["wkva"], w["g_kva"], w["wkvb"], tab_mla, seq)
    o_mla = _mla_flash(qt, k, vt, batch, seq)

    dqt = _dq_proj(x, w["wdq"], tab_diff, seq)
    dk = _dk_proj(x, w["wdk"], tab_diff, seq)
    dvt = _dv_proj(x, w["wdv"])
    o_diff = _diff_flash(dqt, dk, dvt, lam_q, lam_k, w["g_sub"], batch, seq, lam_init)

    gates = _gates(x, w["wgate"], w["b_gate"])
    merged = _merge(o_mla, o_diff, gates, w["w_br_mla"], w["w_br_diff"])
    x1 = _outproj_ln(x, merged, w["w_out"], w["ln1_g"], w["ln1_b"])
    y = _ffn_ln(x1, w["w_ff1"], w["w_ff2"], w["ln2_g"], w["ln2_b"])
    return y.reshape(batch, seq, D_MODEL)


def kernel(x_prompt, x_sample, w_in, b_gate, g_qa, w_qb, g_kva, w_kvb, lam_q, lam_k, g_sub,
           w_br_mla, w_br_diff, w_out, ln1_g, ln1_b, w_ff1, w_ff2, ln2_g, ln2_b):
    outs = [x_prompt, x_sample]
    max_seq = max(x3.shape[1] for x3 in outs)
    tab_mla = jnp.pad(_rope_tables(max_seq, MLA_ROPE, MLA_ROPE), ((0, 0), (0, 0), (0, LANES - MLA_ROPE)))
    tab_diff = jnp.tile(_rope_tables(max_seq, DIFF_ROT, DIFF_QK), (1, 1, LANES // DIFF_QK))
    for l in range(DEPTH):
        lam_init = 0.8 - 0.6 * math.exp(-0.3 * l)
        w = _prep_weights(w_in, b_gate, g_qa, w_qb, g_kva, w_kvb, g_sub, w_br_mla, w_br_diff, w_out,
                          ln1_g, ln1_b, w_ff1, w_ff2, ln2_g, ln2_b, l)
        outs = [_layer(x3, w, tab_mla, tab_diff, lam_q[l], lam_k[l], lam_init) for x3 in outs]
    return tuple(outs)
```

```python
import functools
import math

import jax
import jax.numpy as jnp
from jax import lax
from jax.experimental import pallas as pl
from jax.experimental.pallas import tpu as pltpu

D_MODEL = 2048
DEPTH = 1
MLA_HEADS = 8
MLA_Q_LORA = 768
MLA_KV_LORA = 512
MLA_NOPE = 128
MLA_ROPE = 64
MLA_V = 128
DIFF_HEADS = 8
DIFF_QK = 64
DIFF_V = 2 * DIFF_QK
DIFF_ROT = DIFF_QK // 4
D_FF = 4 * D_MODEL
ROPE_THETA = 500000.0
LN_EPS = 1e-5
RMS_EPS = 1e-6
DN_ALPHA = (2.0 * DEPTH) ** 0.25
LOG2_E = math.log2(math.e)

C_QA = MLA_Q_LORA
C_KVA = MLA_KV_LORA + MLA_ROPE
C_DQ = DIFF_HEADS * 2 * DIFF_QK
C_DK = DIFF_HEADS * 2 * DIFF_QK
C_DV = DIFF_HEADS * DIFF_V

LANES = 128
HEAD_PAD = 2 * LANES
MLA_KEY_CHUNK = 2048
DIFF_KEY_CHUNK = 2048
KEY_TILE = 1024
PV_SPAN = 1024
SUBLANES = 8
VT_TM = 512
PROJ_TM = 1024
MLA_BQ = 512
DIFF_BQ = 256
VMEM_LIMIT = 56 * 1024 * 1024

F32 = jnp.float32
BF16 = jnp.bfloat16


def _params(sem):
    return pltpu.CompilerParams(dimension_semantics=sem, vmem_limit_bytes=VMEM_LIMIT)


def _rope_mix(x, tab_ref, shift):
    n = x.shape[-1]
    fwd = pltpu.roll(x, n - shift, 1)
    bwd = pltpu.roll(x, shift, 1)
    return x * tab_ref[0] + fwd * tab_ref[1] + bwd * tab_ref[2]


def _rmsnorm_rows(x, g):
    ms = jnp.mean(x * x, axis=-1, keepdims=True)
    return x * lax.rsqrt(ms + RMS_EPS) * g


def _layernorm_rows(z, g, b):
    mu = jnp.mean(z, axis=-1, keepdims=True)
    zc = z - mu
    var = jnp.mean(zc * zc, axis=-1, keepdims=True)
    return zc * lax.rsqrt(var + LN_EPS) * g + b


def _q_proj_kernel(x_ref, wqa_ref, gqa_ref, wqb_ref, tab_ref, qt_ref, *, scale):
    xb = x_ref[...].astype(BF16)
    qa = jnp.dot(xb, wqa_ref[...], preferred_element_type=F32)
    qn = _rmsnorm_rows(qa, gqa_ref[...]).astype(BF16)
    q = jnp.dot(qn, wqb_ref[...], preferred_element_type=F32)
    nope_w = MLA_HEADS * MLA_NOPE
    for h in range(MLA_HEADS):
        nope = q[:, h * MLA_NOPE:(h + 1) * MLA_NOPE]
        rope = _rope_mix(q[:, nope_w + h * LANES: nope_w + (h + 1) * LANES], tab_ref, MLA_ROPE // 2)
        nope_t = (nope * scale).T.astype(BF16)
        rope_t = (rope * scale).T.astype(BF16)
        for blk in range(qt_ref.shape[1]):
            cols = slice(blk * MLA_BQ, (blk + 1) * MLA_BQ)
            qt_ref[h, blk, :LANES, :] = nope_t[:, cols]
            qt_ref[h, blk, LANES:, :] = rope_t[:, cols]


def _kv_proj_kernel(x_ref, wkva_ref, gkva_ref, wkvb_ref, tab_ref, k_ref, vt_ref):
    xb = x_ref[...].astype(BF16)
    kva = jnp.dot(xb, wkva_ref[...], preferred_element_type=F32)
    ckv = _rmsnorm_rows(kva[:, :MLA_KV_LORA], gkva_ref[...]).astype(BF16)
    krope = _rope_mix(kva[:, MLA_KV_LORA:], tab_ref, MLA_ROPE // 2).astype(BF16)
    kv = jnp.dot(ckv, wkvb_ref[...], preferred_element_type=F32)
    nope_w = MLA_HEADS * MLA_NOPE
    for h in range(MLA_HEADS):
        k_ref[:, h * HEAD_PAD: h * HEAD_PAD + LANES] = kv[:, h * MLA_NOPE:(h + 1) * MLA_NOPE].astype(BF16)
        k_ref[:, h * HEAD_PAD + LANES:(h + 1) * HEAD_PAD] = krope
    vt = kv[:, nope_w:].T
    vt_ref[:, 0] = vt.reshape(MLA_HEADS, MLA_V, vt.shape[-1]).astype(BF16)


def _dq_proj_kernel(x_ref, w_ref, tab_ref, qt_ref, *, scale):
    xb = x_ref[...].astype(BF16)
    dq = jnp.dot(xb, w_ref[...], preferred_element_type=F32)
    first_map = lax.broadcasted_iota(jnp.int32, (dq.shape[0], LANES), 1) < DIFF_QK
    for h in range(DIFF_HEADS):
        r = _rope_mix(dq[:, h * LANES:(h + 1) * LANES], tab_ref, DIFF_ROT // 2) * scale
        r1 = jnp.where(first_map, r, 0.0).T.astype(BF16)
        r2 = jnp.where(first_map, 0.0, r).T.astype(BF16)
        for blk in range(qt_ref.shape[1]):
            cols = slice(blk * DIFF_BQ, (blk + 1) * DIFF_BQ)
            qt_ref[h, blk, :, :DIFF_BQ] = r1[:, cols]
            qt_ref[h, blk, :, DIFF_BQ:] = r2[:, cols]


def _dk_proj_kernel(x_ref, w_ref, tab_ref, k_ref):
    xb = x_ref[...].astype(BF16)
    dk = jnp.dot(xb, w_ref[...], preferred_element_type=F32)
    for h in range(DIFF_HEADS):
        k_ref[:, h * LANES:(h + 1) * LANES] = _rope_mix(
            dk[:, h * LANES:(h + 1) * LANES], tab_ref, DIFF_ROT // 2).astype(BF16)


def _dv_proj_kernel(x_ref, w_ref, vt_ref):
    xb = x_ref[...].astype(BF16)
    dv = jnp.dot(xb, w_ref[...], preferred_element_type=F32)
    vt = dv.T
    vt_ref[:, 0] = vt.reshape(DIFF_HEADS, DIFF_V, vt.shape[-1]).astype(BF16)


def _gate_kernel(x_ref, w_ref, b_ref, g_ref):
    xb = x_ref[...].astype(BF16)
    z = jnp.dot(xb, w_ref[...], preferred_element_type=F32) + b_ref[...]
    g_ref[...] = (1.0 / (1.0 + jnp.exp(-z))).astype(g_ref.dtype)


def _row_spec(tm, width):
    return pl.BlockSpec((tm, width), lambda i: (i, 0))


def _full_spec(shape):
    nd = len(shape)
    return pl.BlockSpec(shape, lambda i: (0,) * nd, pipeline_mode=pl.Buffered(1))


def _tab_spec(tm, s_tiles):
    return pl.BlockSpec((3, tm, LANES), lambda i: (0, i % s_tiles, 0))


def _vt_out(t, tm, key_chunk):
    per_chunk = key_chunk // tm
    shape = jax.ShapeDtypeStruct((MLA_HEADS, t // key_chunk, MLA_V, key_chunk), BF16)
    spec = pl.BlockSpec((MLA_HEADS, 1, MLA_V, tm), lambda i: (0, i // per_chunk, 0, i % per_chunk))
    return shape, spec


def _q_proj(x, wqa, gqa, wqb, tab, seq):
    t, tm = x.shape[0], PROJ_TM
    blocks = tm // MLA_BQ
    return pl.pallas_call(
        functools.partial(_q_proj_kernel, scale=LOG2_E * (MLA_NOPE + MLA_ROPE) ** -0.5),
        grid=(t // tm,),
        in_specs=[_row_spec(tm, D_MODEL), _full_spec(wqa.shape), _full_spec(gqa.shape),
                  _full_spec(wqb.shape), _tab_spec(tm, seq // tm)],
        out_specs=pl.BlockSpec((MLA_HEADS, blocks, HEAD_PAD, MLA_BQ), lambda i: (0, i, 0, 0)),
        out_shape=jax.ShapeDtypeStruct((MLA_HEADS, t // MLA_BQ, HEAD_PAD, MLA_BQ), BF16),
        compiler_params=_params(("parallel",)),
        name="mla_q_proj",
    )(x, wqa, gqa, wqb, tab)


def _kv_proj(x, wkva, gkva, wkvb, tab, seq):
    t, tm = x.shape[0], VT_TM
    vt_shape, vt_spec = _vt_out(t, tm, MLA_KEY_CHUNK)
    return pl.pallas_call(
        _kv_proj_kernel,
        grid=(t // tm,),
        in_specs=[_row_spec(tm, D_MODEL), _full_spec(wkva.shape), _full_spec(gkva.shape),
                  _full_spec(wkvb.shape), _tab_spec(tm, seq // tm)],
        out_specs=[_row_spec(tm, MLA_HEADS * HEAD_PAD), vt_spec],
        out_shape=[jax.ShapeDtypeStruct((t, MLA_HEADS * HEAD_PAD), BF16), vt_shape],
        compiler_params=_params(("parallel",)),
        name="mla_kv_proj",
    )(x, wkva, gkva, wkvb, tab)


def _dq_proj(x, w, tab, seq):
    t, tm = x.shape[0], PROJ_TM
    blocks = tm // DIFF_BQ
    return pl.pallas_call(
        functools.partial(_dq_proj_kernel, scale=LOG2_E * DIFF_QK ** -0.5),
        grid=(t // tm,),
        in_specs=[_row_spec(tm, D_MODEL), _full_spec(w.shape), _tab_spec(tm, seq // tm)],
        out_specs=pl.BlockSpec((DIFF_HEADS, blocks, LANES, 2 * DIFF_BQ), lambda i: (0, i, 0, 0)),
        out_shape=jax.ShapeDtypeStruct((DIFF_HEADS, t // DIFF_BQ, LANES, 2 * DIFF_BQ), BF16),
        compiler_params=_params(("parallel",)),
        name="diff_q_proj",
    )(x, w, tab)


def _dk_proj(x, w, tab, seq):
    t, tm = x.shape[0], PROJ_TM
    return pl.pallas_call(
        _dk_proj_kernel,
        grid=(t // tm,),
        in_specs=[_row_spec(tm, D_MODEL), _full_spec(w.shape), _tab_spec(tm, seq // tm)],
        out_specs=_row_spec(tm, C_DK),
        out_shape=jax.ShapeDtypeStruct((t, C_DK), BF16),
        compiler_params=_params(("parallel",)),
        name="diff_k_proj",
    )(x, w, tab)


def _dv_proj(x, w):
    t, tm = x.shape[0], PROJ_TM
    vt_shape, vt_spec = _vt_out(t, tm, DIFF_KEY_CHUNK)
    return pl.pallas_call(
        _dv_proj_kernel,
        grid=(t // tm,),
        in_specs=[_row_spec(tm, D_MODEL), _full_spec(w.shape)],
        out_specs=vt_spec,
        out_shape=vt_shape,
        compiler_params=_params(("parallel",)),
        name="diff_v_proj",
    )(x, w)


def _gates(x, w, b):
    t, tm = x.shape[0], PROJ_TM
    n = w.shape[1]
    tn = D_MODEL
    return pl.pallas_call(
        _gate_kernel,
        grid=(n // tn, t // tm),
        in_specs=[pl.BlockSpec((tm, D_MODEL), lambda j, i: (i, 0)),
                  pl.BlockSpec((D_MODEL, tn), lambda j, i: (0, j), pipeline_mode=pl.Buffered(1)),
                  pl.BlockSpec((1, tn), lambda j, i: (0, j))],
        out_specs=pl.BlockSpec((tm, tn), lambda j, i: (i, j)),
        out_shape=jax.ShapeDtypeStruct((t, n), BF16),
        compiler_params=_params(("parallel", "parallel")),
        name="gates",
    )(x, w, b)


def _flash_head(qt_ref, k_ref, vt_ref, s_ref, p_ref, m_ref, l_ref, acc_ref):
    nq, _, nc = qt_ref.shape
    n_chunks, _, key_chunk = vt_ref.shape
    n_steps = n_chunks * nq
    n_tiles = key_chunk // KEY_TILE
    block_bits = nq.bit_length() - 1
    assert nq == 1 << block_bits and n_steps % 2 == 0

    def chunk_and_block(g):
        return lax.shift_right_logical(g, block_bits), lax.bitwise_and(g, nq - 1)

    def sublane_groups(x):
        return x.reshape(KEY_TILE // SUBLANES, SUBLANES, nc)

    def score_tile(g, t):
        chunk, blk = chunk_and_block(g)
        start = pl.multiple_of(chunk * key_chunk + t * KEY_TILE, KEY_TILE)
        return jnp.dot(k_ref[pl.ds(start, KEY_TILE), :], qt_ref[blk], preferred_element_type=F32)

    def add_pv(g_prev, alpha_prev):
        chunk, blk = chunk_and_block(g_prev)
        pv = None
        for h in range(key_chunk // PV_SPAN):
            span = slice(h * PV_SPAN, (h + 1) * PV_SPAN)
            part = jnp.dot(vt_ref[chunk, :, span], p_ref[span, :], preferred_element_type=F32)
            pv = part if h == 0 else pv + part
        acc_ref[blk] = alpha_prev * acc_ref[blk] + pv

    def step(g, carry):
        alpha_prev, s_max = carry
        _, blk = chunk_and_block(g)
        m = m_ref[blk]
        m_new = jnp.maximum(m, s_max)
        alpha = jnp.exp2(m - m_new)
        m_ref[blk] = m_new
        add_pv(jnp.maximum(g - 1, 0), alpha_prev)
        g_next = jnp.minimum(g + 1, n_steps - 1)
        next_max = None
        p_sum = None
        for t in range(n_tiles):
            rows = pl.ds(t * KEY_TILE, KEY_TILE)
            p = jnp.exp2(s_ref[rows, :] - m_new)
            p_ref[rows, :] = p.astype(BF16)
            tile_sum = jnp.sum(sublane_groups(p), axis=0)
            p_sum = tile_sum if t == 0 else p_sum + tile_sum
            s_tile = score_tile(g_next, t)
            s_ref[rows, :] = s_tile
            tile_max = jnp.max(sublane_groups(s_tile), axis=0)
            next_max = tile_max if t == 0 else jnp.maximum(next_max, tile_max)
        l_ref[blk] = alpha * l_ref[blk] + jnp.sum(p_sum, axis=0, keepdims=True)
        return alpha, jnp.max(next_max, axis=0, keepdims=True)

    first_max = None
    for t in range(n_tiles):
        s_tile = score_tile(0, t)
        s_ref[pl.ds(t * KEY_TILE, KEY_TILE), :] = s_tile
        tile_max = jnp.max(sublane_groups(s_tile), axis=0)
        first_max = tile_max if t == 0 else jnp.maximum(first_max, tile_max)
    p_ref[...] = jnp.zeros(p_ref.shape, BF16)
    m_ref[...] = jnp.full(m_ref.shape, -jnp.inf, F32)
    l_ref[...] = jnp.zeros(l_ref.shape, F32)
    acc_ref[...] = jnp.zeros(acc_ref.shape, F32)

    init = (jnp.ones((1, nc), F32), jnp.max(first_max, axis=0, keepdims=True))
    alpha_last, _ = lax.fori_loop(0, n_steps, step, init, unroll=2)
    add_pv(n_steps - 1, alpha_last)


def _mla_flash_kernel(qt_ref, k_ref, vt_ref, o_ref, s_ref, p_ref, m_ref, l_ref, acc_ref):
    _flash_head(qt_ref, k_ref, vt_ref, s_ref, p_ref, m_ref, l_ref, acc_ref)
    nq, _, nc = qt_ref.shape

    def finish(blk, _):
        o = acc_ref[blk] * (1.0 / l_ref[blk])
        o_ref[pl.ds(pl.multiple_of(blk * nc, nc), nc), :] = o.T.astype(BF16)
        return 0

    lax.fori_loop(0, nq, finish, 0)


def _diff_flash_kernel(qt_ref, k_ref, vt_ref, lq_ref, lk_ref, gsub_ref, o_ref,
                       s_ref, p_ref, m_ref, l_ref, acc_ref, *, lam_init):
    _flash_head(qt_ref, k_ref, vt_ref, s_ref, p_ref, m_ref, l_ref, acc_ref)
    nq, _, nc = qt_ref.shape
    bq = nc // 2
    lam_dot = jnp.sum(lq_ref[...] * lk_ref[...], axis=-1, keepdims=True)
    lam_exp = jnp.exp(lam_dot)
    lam = lam_exp[0:1, :] - lam_exp[1:2, :] + lam_init

    def finish(blk, _):
        on = acc_ref[blk] * (1.0 / l_ref[blk])
        a = on[:, :bq] - lam * on[:, bq:]
        ms = jnp.mean(a * a, axis=0, keepdims=True)
        y = (a * lax.rsqrt(ms + RMS_EPS)).T * gsub_ref[...]
        o_ref[pl.ds(pl.multiple_of(blk * bq, bq), bq), :] = (y * (1.0 - lam_init)).astype(BF16)
        return 0

    lax.fori_loop(0, nq, finish, 0)


def _flash_call(kernel, qt, k, vt, extra, batch, seq, dk, dv, nc, key_chunk, name):
    heads = qt.shape[0]
    nq = qt.shape[1] // batch
    n_chunks = seq // key_chunk
    in_bytes = 2 * (nq * dk * nc + seq * dk + seq * dv)
    other_bytes = key_chunk * nc * (4 + 2) + nq * (dv + 2) * nc * 4 + 2 * seq * dv * 2
    once = None if 2 * in_bytes + other_bytes <= VMEM_LIMIT // 2 else pl.Buffered(1)
    small = lambda shape: pl.BlockSpec(shape, lambda b, h: (0, 0))
    return pl.pallas_call(
        kernel,
        grid=(batch, heads),
        in_specs=[pl.BlockSpec((None, nq, dk, nc), lambda b, h: (h, b, 0, 0), pipeline_mode=once),
                  pl.BlockSpec((seq, dk), lambda b, h: (b, h), pipeline_mode=once),
                  pl.BlockSpec((None, n_chunks, dv, key_chunk), lambda b, h: (h, b, 0, 0),
                               pipeline_mode=once)] + [small(e.shape) for e in extra],
        out_specs=pl.BlockSpec((seq, dv), lambda b, h: (b, h)),
        out_shape=jax.ShapeDtypeStruct((batch * seq, heads * dv), BF16),
        scratch_shapes=[pltpu.VMEM((key_chunk, nc), F32), pltpu.VMEM((key_chunk, nc), BF16),
                        pltpu.VMEM((nq, 1, nc), F32), pltpu.VMEM((nq, 1, nc), F32),
                        pltpu.VMEM((nq, dv, nc), F32)],
        compiler_params=_params(("parallel", "parallel")),
        name=name,
    )(qt, k, vt, *extra)


def _mla_flash(qt, k, vt, batch, seq):
    return _flash_call(_mla_flash_kernel, qt, k, vt, (), batch, seq, HEAD_PAD, MLA_V, MLA_BQ,
                       MLA_KEY_CHUNK, "mla_flash")


def _diff_flash(qt, k, vt, lam_q, lam_k, g_sub, batch, seq, lam_init):
    return _flash_call(functools.partial(_diff_flash_kernel, lam_init=lam_init), qt, k, vt,
                       (lam_q, lam_k, g_sub), batch, seq, LANES, DIFF_V, 2 * DIFF_BQ, DIFF_KEY_CHUNK, "diff_flash")


def _merge_kernel(om_ref, od_ref, gm_ref, gd_ref, wm_ref, wd_ref, o_ref):
    a = jnp.dot(om_ref[...], wm_ref[...], preferred_element_type=F32)
    b = jnp.dot(od_ref[...], wd_ref[...], preferred_element_type=F32)
    o_ref[...] = (gm_ref[...] * a + gd_ref[...] * b).astype(BF16)


def _outproj_ln_kernel(x_ref, m_ref, w_ref, g_ref, b_ref, o_ref):
    h = jnp.dot(m_ref[...], w_ref[...], preferred_element_type=F32)
    o_ref[...] = _layernorm_rows(DN_ALPHA * x_ref[...] + h, g_ref[...], b_ref[...])


def _ffn_ln_kernel(x_ref, w1_ref, w2_ref, g_ref, b_ref, o_ref, xb_ref, acc_ref):
    j = pl.program_id(1)

    @pl.when(j == 0)
    def _():
        xb_ref[...] = x_ref[...].astype(BF16)
        acc_ref[...] = jnp.zeros_like(acc_ref)

    h = jnp.maximum(jnp.dot(xb_ref[...], w1_ref[...], preferred_element_type=F32), 0.0)
    acc_ref[...] += jnp.dot((h * h).astype(BF16), w2_ref[...], preferred_element_type=F32)

    @pl.when(j == pl.num_programs(1) - 1)
    def _():
        o_ref[...] = _layernorm_rows(DN_ALPHA * x_ref[...] + acc_ref[...], g_ref[...], b_ref[...])


def _merge(o_mla, o_diff, gates, w_br_mla, w_br_diff):
    t, tm = o_mla.shape[0], 512
    return pl.pallas_call(
        _merge_kernel,
        grid=(t // tm,),
        in_specs=[_row_spec(tm, o_mla.shape[1]), _row_spec(tm, o_diff.shape[1]),
                  pl.BlockSpec((tm, D_MODEL), lambda i: (i, 0)),
                  pl.BlockSpec((tm, D_MODEL), lambda i: (i, 1)),
                  _full_spec(w_br_mla.shape), _full_spec(w_br_diff.shape)],
        out_specs=_row_spec(tm, D_MODEL),
        out_shape=jax.ShapeDtypeStruct((t, D_MODEL), BF16),
        compiler_params=_params(("parallel",)),
        name="branch_merge",
    )(o_mla, o_diff, gates, gates, w_br_mla, w_br_diff)


def _outproj_ln(x, merged, w_out, g, b):
    t, tm = x.shape[0], 512
    return pl.pallas_call(
        _outproj_ln_kernel,
        grid=(t // tm,),
        in_specs=[_row_spec(tm, D_MODEL), _row_spec(tm, D_MODEL), _full_spec(w_out.shape),
                  _full_spec(g.shape), _full_spec(b.shape)],
        out_specs=_row_spec(tm, D_MODEL),
        out_shape=jax.ShapeDtypeStruct((t, D_MODEL), F32),
        compiler_params=_params(("parallel",)),
        name="outproj_ln",
    )(x, merged, w_out, g, b)


def _ffn_ln(x, w1, w2, g, b):
    t, tm, tf = x.shape[0], 512, 1024
    return pl.pallas_call(
        _ffn_ln_kernel,
        grid=(t // tm, D_FF // tf),
        in_specs=[pl.BlockSpec((tm, D_MODEL), lambda i, j: (i, 0)),
                  pl.BlockSpec((D_MODEL, tf), lambda i, j: (0, j)),
                  pl.BlockSpec((tf, D_MODEL), lambda i, j: (j, 0)),
                  pl.BlockSpec((1, D_MODEL), lambda i, j: (0, 0)),
                  pl.BlockSpec((1, D_MODEL), lambda i, j: (0, 0))],
        out_specs=pl.BlockSpec((tm, D_MODEL), lambda i, j: (i, 0)),
        out_shape=jax.ShapeDtypeStruct((t, D_MODEL), F32),
        scratch_shapes=[pltpu.VMEM((tm, D_MODEL), BF16), pltpu.VMEM((tm, D_MODEL), F32)],
        compiler_params=_params(("parallel", "arbitrary")),
        name="ffn_ln",
    )(x, w1, w2, g, b)


def _rope_tables(seq, rot_dim, group):
    half = rot_dim // 2
    inv_freq = ROPE_THETA ** (-jnp.arange(0, rot_dim, 2, dtype=F32) / rot_dim)
    ang = jnp.arange(seq, dtype=F32)[:, None] * inv_freq[None, :]
    cos, sin = jnp.cos(ang), jnp.sin(ang)
    zeros = lambda n: jnp.zeros((seq, n), F32)
    c = jnp.concatenate([cos, cos, jnp.ones((seq, group - rot_dim), F32)], axis=1)
    s_fwd = jnp.concatenate([-sin, zeros(group - half)], axis=1)
    s_bwd = jnp.concatenate([zeros(half), sin, zeros(group - rot_dim)], axis=1)
    return jnp.stack([c, s_fwd, s_bwd])


def _prep_weights(w_in, b_gate, g_qa, w_qb, g_kva, w_kvb, g_sub, w_br_mla, w_br_diff, w_out,
                  ln1_g, ln1_b, w_ff1, w_ff2, ln2_g, ln2_b, l):
    c0 = C_QA
    c1 = c0 + C_KVA
    c2 = c1 + C_DQ
    c3 = c2 + C_DK
    c4 = c3 + C_DV
    wi = w_in[l]
    qb = w_qb[l].reshape(MLA_Q_LORA, MLA_HEADS, MLA_NOPE + MLA_ROPE)
    qb_nope = qb[:, :, :MLA_NOPE].reshape(MLA_Q_LORA, MLA_HEADS * MLA_NOPE)
    qb_rope = jnp.pad(qb[:, :, MLA_NOPE:], ((0, 0), (0, 0), (0, LANES - MLA_ROPE)))
    qb_rope = qb_rope.reshape(MLA_Q_LORA, MLA_HEADS * LANES)
    kvb = w_kvb[l].reshape(MLA_KV_LORA, MLA_HEADS, MLA_NOPE + MLA_V)
    kvb_k = kvb[:, :, :MLA_NOPE].reshape(MLA_KV_LORA, MLA_HEADS * MLA_NOPE)
    kvb_v = kvb[:, :, MLA_NOPE:].reshape(MLA_KV_LORA, MLA_HEADS * MLA_V)
    row = lambda v: v[l].reshape(1, -1)
    return dict(
        wqa=wi[:, :c0].astype(BF16),
        wkva=jnp.pad(wi[:, c0:c1], ((0, 0), (0, LANES - MLA_ROPE))).astype(BF16),
        wdq=wi[:, c1:c2].astype(BF16),
        wdk=wi[:, c2:c3].astype(BF16),
        wdv=wi[:, c3:c4].astype(BF16),
        wgate=wi[:, c4:].astype(BF16),
        b_gate=row(b_gate),
        g_qa=row(g_qa),
        wqb=jnp.concatenate([qb_nope, qb_rope], axis=1).astype(BF16),
        g_kva=row(g_kva),
        wkvb=jnp.concatenate([kvb_k, kvb_v], axis=1).astype(BF16),
        g_sub=row(g_sub),
        w_br_mla=w_br_mla[l].astype(BF16),
        w_br_diff=w_br_diff[l].astype(BF16),
        w_out=w_out[l].astype(BF16),
        ln1_g=row(ln1_g), ln1_b=row(ln1_b),
        w_ff1=w_ff1[l].astype(BF16), w_ff2=w_ff2[l].astype(BF16),
        ln2_g=row(ln2_g), ln2_b=row(ln2_b),
    )


def _layer(x3, w, tab_mla, tab_diff, lam_q, lam_k, lam_init):
    batch, seq, _ = x3.shape
    x = x3.reshape(batch * seq, D_MODEL)

    qt = _q_proj(x, w["wqa"], w["g_qa"], w["wqb"], tab_mla, seq)
    k, vt = _kv_proj(x, w["wkva"], w["g_kva"], w["wkvb"], tab_mla, seq)
    o_mla = _mla_flash(qt, k, vt, batch, seq)

    dqt = _dq_proj(x, w["wdq"], tab_diff, seq)
    dk = _dk_proj(x, w["wdk"], tab_diff, seq)
    dvt = _dv_proj(x, w["wdv"])
    o_diff = _diff_flash(dqt, dk, dvt, lam_q, lam_k, w["g_sub"], batch, seq, lam_init)

    gates = _gates(x, w["wgate"], w["b_gate"])
    merged = _merge(o_mla, o_diff, gates, w["w_br_mla"], w["w_br_diff"])
    x1 = _outproj_ln(x, merged, w["w_out"], w["ln1_g"], w["ln1_b"])
    y = _ffn_ln(x1, w["w_ff1"], w["w_ff2"], w["ln2_g"], w["ln2_b"])
    return y.reshape(batch, seq, D_MODEL)


def kernel(x_prompt, x_sample, w_in, b_gate, g_qa, w_qb, g_kva, w_kvb, lam_q, lam_k, g_sub,
           w_br_mla, w_br_diff, w_out, ln1_g, ln1_b, w_ff1, w_ff2, ln2_g, ln2_b):
    outs = [x_prompt, x_sample]
    max_seq = max(x3.shape[1] for x3 in outs)
    tab_mla = jnp.pad(_rope_tables(max_seq, MLA_ROPE, MLA_ROPE), ((0, 0), (0, 0), (0, LANES - MLA_ROPE)))
    tab_diff = jnp.tile(_rope_tables(max_seq, DIFF_ROT, DIFF_QK), (1, 1, LANES // DIFF_QK))
    for l in range(DEPTH):
        lam_init = 0.8 - 0.6 * math.exp(-0.3 * l)
        w = _prep_weights(w_in, b_gate, g_qa, w_qb, g_kva, w_kvb, g_sub, w_br_mla, w_br_diff, w_out,
                          ln1_g, ln1_b, w_ff1, w_ff2, ln2_g, ln2_b, l)
        outs = [_layer(x3, w, tab_mla, tab_diff, lam_q[l], lam_k[l], lam_init) for x3 in outs]
    return tuple(outs)
```

```python
import functools
import math

import jax
import jax.numpy as jnp
from jax import lax
from jax.experimental import pallas as pl
from jax.experimental.pallas import tpu as pltpu

D_MODEL = 2048
DEPTH = 1
MLA_HEADS = 8
MLA_Q_LORA = 768
MLA_KV_LORA = 512
MLA_NOPE = 128
MLA_ROPE = 64
MLA_V = 128
DIFF_HEADS = 8
DIFF_QK = 64
DIFF_V = 2 * DIFF_QK
DIFF_ROT = DIFF_QK // 4
D_FF = 4 * D_MODEL
ROPE_THETA = 500000.0
LN_EPS = 1e-5
RMS_EPS = 1e-6
DN_ALPHA = (2.0 * DEPTH) ** 0.25
LOG2_E = math.log2(math.e)

C_QA = MLA_Q_LORA
C_KVA = MLA_KV_LORA + MLA_ROPE
C_DQ = DIFF_HEADS * 2 * DIFF_QK
C_DK = DIFF_HEADS * 2 * DIFF_QK
C_DV = DIFF_HEADS * DIFF_V

LANES = 128
HEAD_PAD = 2 * LANES
MLA_KEY_CHUNK = 2048
DIFF_KEY_CHUNK = 2048
KEY_TILE = 1024
PV_SPAN = 1024
SUBLANES = 8
VT_TM = 512
PROJ_TM = 1024
MIX_TM = 512
FFN_TF = 1024
MLA_BQ = 512
DIFF_BQ = 256
VMEM_LIMIT = 56 * 1024 * 1024

F32 = jnp.float32
BF16 = jnp.bfloat16


def _params(sem):
    return pltpu.CompilerParams(dimension_semantics=sem, vmem_limit_bytes=VMEM_LIMIT)


def _rope_mix(x, tab_ref, shift):
    n = x.shape[-1]
    fwd = pltpu.roll(x, n - shift, 1)
    bwd = pltpu.roll(x, shift, 1)
    return x * tab_ref[0] + fwd * tab_ref[1] + bwd * tab_ref[2]


def _rmsnorm_rows(x, g):
    ms = jnp.mean(x * x, axis=-1, keepdims=True)
    return x * lax.rsqrt(ms + RMS_EPS) * g


def _layernorm_rows(z, g, b):
    mu = jnp.mean(z, axis=-1, keepdims=True)
    zc = z - mu
    var = jnp.mean(zc * zc, axis=-1, keepdims=True)
    return zc * lax.rsqrt(var + LN_EPS) * g + b


def _q_proj_kernel(x_ref, wqa_ref, gqa_ref, wqb_ref, tab_ref, qt_ref, *, scale):
    xb = x_ref[...].astype(BF16)
    qa = jnp.dot(xb, wqa_ref[...], preferred_element_type=F32)
    qn = _rmsnorm_rows(qa, gqa_ref[...]).astype(BF16)
    q = jnp.dot(qn, wqb_ref[...], preferred_element_type=F32)
    nope_w = MLA_HEADS * MLA_NOPE
    for h in range(MLA_HEADS):
        nope = q[:, h * MLA_NOPE:(h + 1) * MLA_NOPE]
        rope = _rope_mix(q[:, nope_w + h * LANES: nope_w + (h + 1) * LANES], tab_ref, MLA_ROPE // 2)
        nope_t = (nope * scale).T.astype(BF16)
        rope_t = (rope * scale).T.astype(BF16)
        for blk in range(qt_ref.shape[1]):
            cols = slice(blk * MLA_BQ, (blk + 1) * MLA_BQ)
            qt_ref[h, blk, :LANES, :] = nope_t[:, cols]
            qt_ref[h, blk, LANES:, :] = rope_t[:, cols]


def _kv_proj_kernel(x_ref, wkva_ref, gkva_ref, wkvb_ref, tab_ref, k_ref, vt_ref):
    xb = x_ref[...].astype(BF16)
    kva = jnp.dot(xb, wkva_ref[...], preferred_element_type=F32)
    ckv = _rmsnorm_rows(kva[:, :MLA_KV_LORA], gkva_ref[...]).astype(BF16)
    krope = _rope_mix(kva[:, MLA_KV_LORA:], tab_ref, MLA_ROPE // 2).astype(BF16)
    kv = jnp.dot(ckv, wkvb_ref[...], preferred_element_type=F32)
    nope_w = MLA_HEADS * MLA_NOPE
    for h in range(MLA_HEADS):
        k_ref[:, h * HEAD_PAD: h * HEAD_PAD + LANES] = kv[:, h * MLA_NOPE:(h + 1) * MLA_NOPE].astype(BF16)
        k_ref[:, h * HEAD_PAD + LANES:(h + 1) * HEAD_PAD] = krope
    vt = kv[:, nope_w:].T
    vt_ref[:, 0] = vt.reshape(MLA_HEADS, MLA_V, vt.shape[-1]).astype(BF16)


def _dq_proj_kernel(x_ref, w_ref, tab_ref, qt_ref, *, scale):
    xb = x_ref[...].astype(BF16)
    dq = jnp.dot(xb, w_ref[...], preferred_element_type=F32)
    first_map = lax.broadcasted_iota(jnp.int32, (dq.shape[0], LANES), 1) < DIFF_QK
    for h in range(DIFF_HEADS):
        r = _rope_mix(dq[:, h * LANES:(h + 1) * LANES], tab_ref, DIFF_ROT // 2) * scale
        r1 = jnp.where(first_map, r, 0.0).T.astype(BF16)
        r2 = jnp.where(first_map, 0.0, r).T.astype(BF16)
        for blk in range(qt_ref.shape[1]):
            cols = slice(blk * DIFF_BQ, (blk + 1) * DIFF_BQ)
            qt_ref[h, blk, :, :DIFF_BQ] = r1[:, cols]
            qt_ref[h, blk, :, DIFF_BQ:] = r2[:, cols]


def _dk_proj_kernel(x_ref, w_ref, tab_ref, k_ref):
    xb = x_ref[...].astype(BF16)
    dk = jnp.dot(xb, w_ref[...], preferred_element_type=F32)
    for h in range(DIFF_HEADS):
        k_ref[:, h * LANES:(h + 1) * LANES] = _rope_mix(
            dk[:, h * LANES:(h + 1) * LANES], tab_ref, DIFF_ROT // 2).astype(BF16)


def _dv_proj_kernel(x_ref, w_ref, vt_ref):
    xb = x_ref[...].astype(BF16)
    dv = jnp.dot(xb, w_ref[...], preferred_element_type=F32)
    vt = dv.T
    vt_ref[:, 0] = vt.reshape(DIFF_HEADS, DIFF_V, vt.shape[-1]).astype(BF16)


def _gate_kernel(x_ref, w_ref, b_ref, g_ref):
    xb = x_ref[...].astype(BF16)
    z = jnp.dot(xb, w_ref[...], preferred_element_type=F32) + b_ref[...]
    g_ref[...] = (1.0 / (1.0 + jnp.exp(-z))).astype(g_ref.dtype)


def _row_spec(tm, width):
    return pl.BlockSpec((tm, width), lambda i: (i, 0))


def _full_spec(shape):
    nd = len(shape)
    return pl.BlockSpec(shape, lambda i: (0,) * nd, pipeline_mode=pl.Buffered(1))


def _tab_spec(tm, s_tiles):
    return pl.BlockSpec((3, tm, LANES), lambda i: (0, i % s_tiles, 0))


def _vt_out(t, tm, key_chunk):
    per_chunk = key_chunk // tm
    shape = jax.ShapeDtypeStruct((MLA_HEADS, t // key_chunk, MLA_V, key_chunk), BF16)
    spec = pl.BlockSpec((MLA_HEADS, 1, MLA_V, tm), lambda i: (0, i // per_chunk, 0, i % per_chunk))
    return shape, spec


def _q_proj(x, wqa, gqa, wqb, tab, seq):
    t, tm = x.shape[0], PROJ_TM
    blocks = tm // MLA_BQ
    return pl.pallas_call(
        functools.partial(_q_proj_kernel, scale=LOG2_E * (MLA_NOPE + MLA_ROPE) ** -0.5),
        grid=(t // tm,),
        in_specs=[_row_spec(tm, D_MODEL), _full_spec(wqa.shape), _full_spec(gqa.shape),
                  _full_spec(wqb.shape), _tab_spec(tm, seq // tm)],
        out_specs=pl.BlockSpec((MLA_HEADS, blocks, HEAD_PAD, MLA_BQ), lambda i: (0, i, 0, 0)),
        out_shape=jax.ShapeDtypeStruct((MLA_HEADS, t // MLA_BQ, HEAD_PAD, MLA_BQ), BF16),
        compiler_params=_params(("parallel",)),
        name="mla_q_proj",
    )(x, wqa, gqa, wqb, tab)


def _kv_proj(x, wkva, gkva, wkvb, tab, seq):
    t, tm = x.shape[0], VT_TM
    vt_shape, vt_spec = _vt_out(t, tm, MLA_KEY_CHUNK)
    return pl.pallas_call(
        _kv_proj_kernel,
        grid=(t // tm,),
        in_specs=[_row_spec(tm, D_MODEL), _full_spec(wkva.shape), _full_spec(gkva.shape),
                  _full_spec(wkvb.shape), _tab_spec(tm, seq // tm)],
        out_specs=[_row_spec(tm, MLA_HEADS * HEAD_PAD), vt_spec],
        out_shape=[jax.ShapeDtypeStruct((t, MLA_HEADS * HEAD_PAD), BF16), vt_shape],
        compiler_params=_params(("parallel",)),
        name="mla_kv_proj",
    )(x, wkva, gkva, wkvb, tab)


def _dq_proj(x, w, tab, seq):
    t, tm = x.shape[0], PROJ_TM
    blocks = tm // DIFF_BQ
    return pl.pallas_call(
        functools.partial(_dq_proj_kernel, scale=LOG2_E * DIFF_QK ** -0.5),
        grid=(t // tm,),
        in_specs=[_row_spec(tm, D_MODEL), _full_spec(w.shape), _tab_spec(tm, seq // tm)],
        out_specs=pl.BlockSpec((DIFF_HEADS, blocks, LANES, 2 * DIFF_BQ), lambda i: (0, i, 0, 0)),
        out_shape=jax.ShapeDtypeStruct((DIFF_HEADS, t // DIFF_BQ, LANES, 2 * DIFF_BQ), BF16),
        compiler_params=_params(("parallel",)),
        name="diff_q_proj",
    )(x, w, tab)


def _dk_proj(x, w, tab, seq):
    t, tm = x.shape[0], PROJ_TM
    return pl.pallas_call(
        _dk_proj_kernel,
        grid=(t // tm,),
        in_specs=[_row_spec(tm, D_MODEL), _full_spec(w.shape), _tab_spec(tm, seq // tm)],
        out_specs=_row_spec(tm, C_DK),
        out_shape=jax.ShapeDtypeStruct((t, C_DK), BF16),
        compiler_params=_params(("parallel",)),
        name="diff_k_proj",
    )(x, w, tab)


def _dv_proj(x, w):
    t, tm = x.shape[0], PROJ_TM
    vt_shape, vt_spec = _vt_out(t, tm, DIFF_KEY_CHUNK)
    return pl.pallas_call(
        _dv_proj_kernel,
        grid=(t // tm,),
        in_specs=[_row_spec(tm, D_MODEL), _full_spec(w.shape)],
        out_specs=vt_spec,
        out_shape=vt_shape,
        compiler_params=_params(("parallel",)),
        name="diff_v_proj",
    )(x, w)


def _gates(x, w, b):
    t, tm = x.shape[0], PROJ_TM
    n = w.shape[1]
    tn = D_MODEL
    return pl.pallas_call(
        _gate_kernel,
        grid=(n // tn, t // tm),
        in_specs=[pl.BlockSpec((tm, D_MODEL), lambda j, i: (i, 0)),
                  pl.BlockSpec((D_MODEL, tn), lambda j, i: (0, j), pipeline_mode=pl.Buffered(1)),
                  pl.BlockSpec((1, tn), lambda j, i: (0, j))],
        out_specs=pl.BlockSpec((tm, tn), lambda j, i: (i, j)),
        out_shape=jax.ShapeDtypeStruct((t, n), BF16),
        compiler_params=_params(("parallel", "parallel")),
        name="gates",
    )(x, w, b)


def _flash_head(qt_ref, k_ref, vt_ref, s_ref, p_ref, m_ref, l_ref, acc_ref):
    nq, _, nc = qt_ref.shape
    n_chunks, _, key_chunk = vt_ref.shape
    n_steps = n_chunks * nq
    n_tiles = key_chunk // KEY_TILE
    block_bits = nq.bit_length() - 1
    assert nq == 1 << block_bits and n_steps % 2 == 0

    def chunk_and_block(g):
        return lax.shift_right_logical(g, block_bits), lax.bitwise_and(g, nq - 1)

    def sublane_groups(x):
        return x.reshape(KEY_TILE // SUBLANES, SUBLANES, nc)

    def score_tile(g, t):
        chunk, blk = chunk_and_block(g)
        start = pl.multiple_of(chunk * key_chunk + t * KEY_TILE, KEY_TILE)
        return jnp.dot(k_ref[pl.ds(start, KEY_TILE), :], qt_ref[blk], preferred_element_type=F32)

    def add_pv(g_prev, alpha_prev):
        chunk, blk = chunk_and_block(g_prev)
        pv = None
        for h in range(key_chunk // PV_SPAN):
            span = slice(h * PV_SPAN, (h + 1) * PV_SPAN)
            part = jnp.dot(vt_ref[chunk, :, span], p_ref[span, :], preferred_element_type=F32)
            pv = part if h == 0 else pv + part
        acc_ref[blk] = alpha_prev * acc_ref[blk] + pv

    def step(g, carry):
        alpha_prev, s_max = carry
        _, blk = chunk_and_block(g)
        m = m_ref[blk]
        m_new = jnp.maximum(m, s_max)
        alpha = jnp.exp2(m - m_new)
        m_ref[blk] = m_new
        add_pv(jnp.maximum(g - 1, 0), alpha_prev)
        g_next = jnp.minimum(g + 1, n_steps - 1)
        next_max = None
        p_sum = None
        for t in range(n_tiles):
            rows = pl.ds(t * KEY_TILE, KEY_TILE)
            p = jnp.exp2(s_ref[rows, :] - m_new)
            p_ref[rows, :] = p.astype(BF16)
            tile_sum = jnp.sum(sublane_groups(p), axis=0)
            p_sum = tile_sum if t == 0 else p_sum + tile_sum
            s_tile = score_tile(g_next, t)
            s_ref[rows, :] = s_tile
            tile_max = jnp.max(sublane_groups(s_tile), axis=0)
            next_max = tile_max if t == 0 else jnp.maximum(next_max, tile_max)
        l_ref[blk] = alpha * l_ref[blk] + jnp.sum(p_sum, axis=0, keepdims=True)
        return alpha, jnp.max(next_max, axis=0, keepdims=True)

    first_max = None
    for t in range(n_tiles):
        s_tile = score_tile(0, t)
        s_ref[pl.ds(t * KEY_TILE, KEY_TILE), :] = s_tile
        tile_max = jnp.max(sublane_groups(s_tile), axis=0)
        first_max = tile_max if t == 0 else jnp.maximum(first_max, tile_max)
    p_ref[...] = jnp.zeros(p_ref.shape, BF16)
    m_ref[...] = jnp.full(m_ref.shape, -jnp.inf, F32)
    l_ref[...] = jnp.zeros(l_ref.shape, F32)
    acc_ref[...] = jnp.zeros(acc_ref.shape, F32)

    init = (jnp.ones((1, nc), F32), jnp.max(first_max, axis=0, keepdims=True))
    alpha_last, _ = lax.fori_loop(0, n_steps, step, init, unroll=2)
    add_pv(n_steps - 1, alpha_last)


def _mla_flash_kernel(qt_ref, k_ref, vt_ref, o_ref, s_ref, p_ref, m_ref, l_ref, acc_ref):
    _flash_head(qt_ref, k_ref, vt_ref, s_ref, p_ref, m_ref, l_ref, acc_ref)
    nq, _, nc = qt_ref.shape

    def finish(blk, _):
        o = acc_ref[blk] * (1.0 / l_ref[blk])
        o_ref[pl.ds(pl.multiple_of(blk * nc, nc), nc), :] = o.T.astype(BF16)
        return 0

    lax.fori_loop(0, nq, finish, 0, unroll=8)


def _diff_flash_kernel(qt_ref, k_ref, vt_ref, lq_ref, lk_ref, gsub_ref, o_ref,
                       s_ref, p_ref, m_ref, l_ref, acc_ref, *, lam_init):
    _flash_head(qt_ref, k_ref, vt_ref, s_ref, p_ref, m_ref, l_ref, acc_ref)
    nq, _, nc = qt_ref.shape
    bq = nc // 2
    lam_dot = jnp.sum(lq_ref[...] * lk_ref[...], axis=-1, keepdims=True)
    lam_exp = jnp.exp(lam_dot)
    lam = lam_exp[0:1, :] - lam_exp[1:2, :] + lam_init

    def finish(blk, _):
        on = acc_ref[blk] * (1.0 / l_ref[blk])
        a = on[:, :bq] - lam * on[:, bq:]
        ms = jnp.mean(a * a, axis=0, keepdims=True)
        y = (a * lax.rsqrt(ms + RMS_EPS)).T * gsub_ref[...]
        o_ref[pl.ds(pl.multiple_of(blk * bq, bq), bq), :] = (y * (1.0 - lam_init)).astype(BF16)
        return 0

    lax.fori_loop(0, nq, finish, 0, unroll=8)


def _flash_call(kernel, qt, k, vt, extra, batch, seq, dk, dv, nc, key_chunk, name):
    heads = qt.shape[0]
    nq = qt.shape[1] // batch
    n_chunks = seq // key_chunk
    in_bytes = 2 * (nq * dk * nc + seq * dk + seq * dv)
    other_bytes = key_chunk * nc * (4 + 2) + nq * (dv + 2) * nc * 4 + 2 * seq * dv * 2
    once = None if 2 * in_bytes + other_bytes <= VMEM_LIMIT else pl.Buffered(1)
    small = lambda shape: pl.BlockSpec(shape, lambda b, h: (0, 0))
    return pl.pallas_call(
        kernel,
        grid=(batch, heads),
        in_specs=[pl.BlockSpec((None, nq, dk, nc), lambda b, h: (h, b, 0, 0), pipeline_mode=once),
                  pl.BlockSpec((seq, dk), lambda b, h: (b, h), pipeline_mode=once),
                  pl.BlockSpec((None, n_chunks, dv, key_chunk), lambda b, h: (h, b, 0, 0),
                               pipeline_mode=once)] + [small(e.shape) for e in extra],
        out_specs=pl.BlockSpec((seq, dv), lambda b, h: (b, h)),
        out_shape=jax.ShapeDtypeStruct((batch * seq, heads * dv), BF16),
        scratch_shapes=[pltpu.VMEM((key_chunk, nc), F32), pltpu.VMEM((key_chunk, nc), BF16),
                        pltpu.VMEM((nq, 1, nc), F32), pltpu.VMEM((nq, 1, nc), F32),
                        pltpu.VMEM((nq, dv, nc), F32)],
        compiler_params=_params(("parallel", "parallel")),
        name=name,
    )(qt, k, vt, *extra)


def _mla_flash(qt, k, vt, batch, seq):
    return _flash_call(_mla_flash_kernel, qt, k, vt, (), batch, seq, HEAD_PAD, MLA_V, MLA_BQ,
                       MLA_KEY_CHUNK, "mla_flash")


def _diff_flash(qt, k, vt, lam_q, lam_k, g_sub, batch, seq, lam_init):
    return _flash_call(functools.partial(_diff_flash_kernel, lam_init=lam_init), qt, k, vt,
                       (lam_q, lam_k, g_sub), batch, seq, LANES, DIFF_V, 2 * DIFF_BQ, DIFF_KEY_CHUNK, "diff_flash")


def _merge_kernel(om_ref, od_ref, gm_ref, gd_ref, wm_ref, wd_ref, o_ref):
    a = jnp.dot(om_ref[...], wm_ref[...], preferred_element_type=F32)
    b = jnp.dot(od_ref[...], wd_ref[...], preferred_element_type=F32)
    o_ref[...] = (gm_ref[...] * a + gd_ref[...] * b).astype(BF16)


def _outproj_ln_kernel(x_ref, m_ref, w_ref, g_ref, b_ref, o_ref):
    h = jnp.dot(m_ref[...], w_ref[...], preferred_element_type=F32)
    o_ref[...] = _layernorm_rows(DN_ALPHA * x_ref[...] + h, g_ref[...], b_ref[...])


def _ffn_ln_kernel(x_ref, w1_ref, w2_ref, g_ref, b_ref, o_ref, xb_ref, acc_ref):
    j = pl.program_id(1)

    @pl.when(j == 0)
    def _():
        xb_ref[...] = x_ref[...].astype(BF16)
        acc_ref[...] = jnp.zeros_like(acc_ref)

    h = jnp.maximum(jnp.dot(xb_ref[...], w1_ref[...], preferred_element_type=F32), 0.0)
    acc_ref[...] += jnp.dot((h * h).astype(BF16), w2_ref[...], preferred_element_type=F32)

    @pl.when(j == pl.num_programs(1) - 1)
    def _():
        o_ref[...] = _layernorm_rows(DN_ALPHA * x_ref[...] + acc_ref[...], g_ref[...], b_ref[...])


def _merge(o_mla, o_diff, gates, w_br_mla, w_br_diff):
    t, tm = o_mla.shape[0], MIX_TM
    return pl.pallas_call(
        _merge_kernel,
        grid=(t // tm,),
        in_specs=[_row_spec(tm, o_mla.shape[1]), _row_spec(tm, o_diff.shape[1]),
                  pl.BlockSpec((tm, D_MODEL), lambda i: (i, 0)),
                  pl.BlockSpec((tm, D_MODEL), lambda i: (i, 1)),
                  _full_spec(w_br_mla.shape), _full_spec(w_br_diff.shape)],
        out_specs=_row_spec(tm, D_MODEL),
        out_shape=jax.ShapeDtypeStruct((t, D_MODEL), BF16),
        compiler_params=_params(("parallel",)),
        name="branch_merge",
    )(o_mla, o_diff, gates, gates, w_br_mla, w_br_diff)


def _outproj_ln(x, merged, w_out, g, b):
    t, tm = x.shape[0], MIX_TM
    return pl.pallas_call(
        _outproj_ln_kernel,
        grid=(t // tm,),
        in_specs=[_row_spec(tm, D_MODEL), _row_spec(tm, D_MODEL), _full_spec(w_out.shape),
                  _full_spec(g.shape), _full_spec(b.shape)],
        out_specs=_row_spec(tm, D_MODEL),
        out_shape=jax.ShapeDtypeStruct((t, D_MODEL), F32),
        compiler_params=_params(("parallel",)),
        name="outproj_ln",
    )(x, merged, w_out, g, b)


def _ffn_ln(x, w1, w2, g, b):
    t, tm, tf = x.shape[0], MIX_TM, FFN_TF
    return pl.pallas_call(
        _ffn_ln_kernel,
        grid=(t // tm, D_FF // tf),
        in_specs=[pl.BlockSpec((tm, D_MODEL), lambda i, j: (i, 0)),
                  pl.BlockSpec((D_MODEL, tf), lambda i, j: (0, j)),
                  pl.BlockSpec((tf, D_MODEL), lambda i, j: (j, 0)),
                  pl.BlockSpec((1, D_MODEL), lambda i, j: (0, 0)),
                  pl.BlockSpec((1, D_MODEL), lambda i, j: (0, 0))],
        out_specs=pl.BlockSpec((tm, D_MODEL), lambda i, j: (i, 0)),
        out_shape=jax.ShapeDtypeStruct((t, D_MODEL), F32),
        scratch_shapes=[pltpu.VMEM((tm, D_MODEL), BF16), pltpu.VMEM((tm, D_MODEL), F32)],
        compiler_params=_params(("parallel", "arbitrary")),
        name="ffn_ln",
    )(x, w1, w2, g, b)


def _rope_tables(seq, rot_dim, group):
    half = rot_dim // 2
    inv_freq = ROPE_THETA ** (-jnp.arange(0, rot_dim, 2, dtype=F32) / rot_dim)
    ang = jnp.arange(seq, dtype=F32)[:, None] * inv_freq[None, :]
    cos, sin = jnp.cos(ang), jnp.sin(ang)
    zeros = lambda n: jnp.zeros((seq, n), F32)
    c = jnp.concatenate([cos, cos, jnp.ones((seq, group - rot_dim), F32)], axis=1)
    s_fwd = jnp.concatenate([-sin, zeros(group - half)], axis=1)
    s_bwd = jnp.concatenate([zeros(half), sin, zeros(group - rot_dim)], axis=1)
    return jnp.stack([c, s_fwd, s_bwd])


def _prep_weights(w_in, b_gate, g_qa, w_qb, g_kva, w_kvb, g_sub, w_br_mla, w_br_diff, w_out,
                  ln1_g, ln1_b, w_ff1, w_ff2, ln2_g, ln2_b, l):
    c0 = C_QA
    c1 = c0 + C_KVA
    c2 = c1 + C_DQ
    c3 = c2 + C_DK
    c4 = c3 + C_DV
    wi = w_in[l]
    qb = w_qb[l].reshape(MLA_Q_LORA, MLA_HEADS, MLA_NOPE + MLA_ROPE)
    qb_nope = qb[:, :, :MLA_NOPE].reshape(MLA_Q_LORA, MLA_HEADS * MLA_NOPE)
    qb_rope = jnp.pad(qb[:, :, MLA_NOPE:], ((0, 0), (0, 0), (0, LANES - MLA_ROPE)))
    qb_rope = qb_rope.reshape(MLA_Q_LORA, MLA_HEADS * LANES)
    kvb = w_kvb[l].reshape(MLA_KV_LORA, MLA_HEADS, MLA_NOPE + MLA_V)
    kvb_k = kvb[:, :, :MLA_NOPE].reshape(MLA_KV_LORA, MLA_HEADS * MLA_NOPE)
    kvb_v = kvb[:, :, MLA_NOPE:].reshape(MLA_KV_LORA, MLA_HEADS * MLA_V)
    row = lambda v: v[l].reshape(1, -1)
    return dict(
        wqa=wi[:, :c0].astype(BF16),
        wkva=jnp.pad(wi[:, c0:c1], ((0, 0), (0, LANES - MLA_ROPE))).astype(BF16),
        wdq=wi[:, c1:c2].astype(BF16),
        wdk=wi[:, c2:c3].astype(BF16),
        wdv=wi[:, c3:c4].astype(BF16),
        wgate=wi[:, c4:].astype(BF16),
        b_gate=row(b_gate),
        g_qa=row(g_qa),
        wqb=jnp.concatenate([qb_nope, qb_rope], axis=1).astype(BF16),
        g_kva=row(g_kva),
        wkvb=jnp.concatenate([kvb_k, kvb_v], axis=1).astype(BF16),
        g_sub=row(g_sub),
        w_br_mla=w_br_mla[l].astype(BF16),
        w_br_diff=w_br_diff[l].astype(BF16),
        w_out=w_out[l].astype(BF16),
        ln1_g=row(ln1_g), ln1_b=row(ln1_b),
        w_ff1=w_ff1[l].astype(BF16), w_ff2=w_ff2[l].astype(BF16),
        ln2_g=row(ln2_g), ln2_b=row(ln2_b),
    )


def _layer(x3, w, tab_mla, tab_diff, lam_q, lam_k, lam_init):
    batch, seq, _ = x3.shape
    x = x3.reshape(batch * seq, D_MODEL)

    qt = _q_proj(x, w["wqa"], w["g_qa"], w["wqb"], tab_mla, seq)
    k, vt = _kv_proj(x, w["wkva"], w["g_kva"], w["wkvb"], tab_mla, seq)
    o_mla = _mla_flash(qt, k, vt, batch, seq)

    dqt = _dq_proj(x, w["wdq"], tab_diff, seq)
    dk = _dk_proj(x, w["wdk"], tab_diff, seq)
    dvt = _dv_proj(x, w["wdv"])
    o_diff = _diff_flash(dqt, dk, dvt, lam_q, lam_k, w["g_sub"], batch, seq, lam_init)

    gates = _gates(x, w["wgate"], w["b_gate"])
    merged = _merge(o_mla, o_diff, gates, w["w_br_mla"], w["w_br_diff"])
    x1 = _outproj_ln(x, merged, w["w_out"], w["ln1_g"], w["ln1_b"])
    y = _ffn_ln(x1, w["w_ff1"], w["w_ff2"], w["ln2_g"], w["ln2_b"])
    return y.reshape(batch, seq, D_MODEL)


def kernel(x_prompt, x_sample, w_in, b_gate, g_qa, w_qb, g_kva, w_kvb, lam_q, lam_k, g_sub,
           w_br_mla, w_br_diff, w_out, ln1_g, ln1_b, w_ff1, w_ff2, ln2_g, ln2_b):
    outs = [x_prompt, x_sample]
    max_seq = max(x3.shape[1] for x3 in outs)
    tab_mla = jnp.pad(_rope_tables(max_seq, MLA_ROPE, MLA_ROPE), ((0, 0), (0, 0), (0, LANES - MLA_ROPE)))
    tab_diff = jnp.tile(_rope_tables(max_seq, DIFF_ROT, DIFF_QK), (1, 1, LANES // DIFF_QK))
    for l in range(DEPTH):
        lam_init = 0.8 - 0.6 * math.exp(-0.3 * l)
        w = _prep_weights(w_in, b_gate, g_qa, w_qb, g_kva, w_kvb, g_sub, w_br_mla, w_br_diff, w_out,
                          ln1_g, ln1_b, w_ff1, w_ff2, ln2_g, ln2_b, l)
        outs = [_layer(x3, w, tab_mla, tab_diff, lam_q[l], lam_k[l], lam_init) for x3 in outs]
    return tuple(outs)
```

```python
import functools
import math

import jax
import jax.numpy as jnp
from jax import lax
from jax.experimental import pallas as pl
from jax.experimental.pallas import tpu as pltpu

D_MODEL = 2048
DEPTH = 1
MLA_HEADS = 8
MLA_Q_LORA = 768
MLA_KV_LORA = 512
MLA_NOPE = 128
MLA_ROPE = 64
MLA_V = 128
DIFF_HEADS = 8
DIFF_QK = 64
DIFF_V = 2 * DIFF_QK
DIFF_ROT = DIFF_QK // 4
D_FF = 4 * D_MODEL
ROPE_THETA = 500000.0
LN_EPS = 1e-5
RMS_EPS = 1e-6
DN_ALPHA = (2.0 * DEPTH) ** 0.25
LOG2_E = math.log2(math.e)

C_QA = MLA_Q_LORA
C_KVA = MLA_KV_LORA + MLA_ROPE
C_DQ = DIFF_HEADS * 2 * DIFF_QK
C_DK = DIFF_HEADS * 2 * DIFF_QK
C_DV = DIFF_HEADS * DIFF_V

LANES = 128
HEAD_PAD = 2 * LANES
MLA_KEY_CHUNK = 2048
DIFF_KEY_CHUNK = 2048
KEY_TILE = 1024
PV_SPAN = 1024
SUBLANES = 8
VT_TM = 512
PROJ_TM = 1024
MIX_TM = 512
FFN_TF = 1024
MLA_BQ = 512
DIFF_BQ = 256
VMEM_LIMIT = 56 * 1024 * 1024

F32 = jnp.float32
BF16 = jnp.bfloat16


def _params(sem):
    return pltpu.CompilerParams(dimension_semantics=sem, vmem_limit_bytes=VMEM_LIMIT)


def _rope_mix(x, tab_ref, shift):
    n = x.shape[-1]
    fwd = pltpu.roll(x, n - shift, 1)
    bwd = pltpu.roll(x, shift, 1)
    return x * tab_ref[0] + fwd * tab_ref[1] + bwd * tab_ref[2]


def _rmsnorm_rows(x, g):
    ms = jnp.mean(x * x, axis=-1, keepdims=True)
    return x * lax.rsqrt(ms + RMS_EPS) * g


def _layernorm_rows(z, g, b):
    mu = jnp.mean(z, axis=-1, keepdims=True)
    zc = z - mu
    var = jnp.mean(zc * zc, axis=-1, keepdims=True)
    return zc * lax.rsqrt(var + LN_EPS) * g + b


def _q_proj_kernel(x_ref, wqa_ref, gqa_ref, wqb_ref, tab_ref, qt_ref, *, scale):
    xb = x_ref[...].astype(BF16)
    qa = jnp.dot(xb, wqa_ref[...], preferred_element_type=F32)
    qn = _rmsnorm_rows(qa, gqa_ref[...]).astype(BF16)
    q = jnp.dot(qn, wqb_ref[...], preferred_element_type=F32)
    nope_w = MLA_HEADS * MLA_NOPE
    for h in range(MLA_HEADS):
        nope = q[:, h * MLA_NOPE:(h + 1) * MLA_NOPE]
        rope = _rope_mix(q[:, nope_w + h * LANES: nope_w + (h + 1) * LANES], tab_ref, MLA_ROPE // 2)
        nope_t = (nope * scale).T.astype(BF16)
        rope_t = (rope * scale).T.astype(BF16)
        for blk in range(qt_ref.shape[1]):
            cols = slice(blk * MLA_BQ, (blk + 1) * MLA_BQ)
            qt_ref[h, blk, :LANES, :] = nope_t[:, cols]
            qt_ref[h, blk, LANES:, :] = rope_t[:, cols]


def _kv_proj_kernel(x_ref, wkva_ref, gkva_ref, wkvb_ref, tab_ref, k_ref, vt_ref):
    xb = x_ref[...].astype(BF16)
    kva = jnp.dot(xb, wkva_ref[...], preferred_element_type=F32)
    ckv = _rmsnorm_rows(kva[:, :MLA_KV_LORA], gkva_ref[...]).astype(BF16)
    krope = _rope_mix(kva[:, MLA_KV_LORA:], tab_ref, MLA_ROPE // 2).astype(BF16)
    kv = jnp.dot(ckv, wkvb_ref[...], preferred_element_type=F32)
    nope_w = MLA_HEADS * MLA_NOPE
    for h in range(MLA_HEADS):
        k_ref[:, h * HEAD_PAD: h * HEAD_PAD + LANES] = kv[:, h * MLA_NOPE:(h + 1) * MLA_NOPE].astype(BF16)
        k_ref[:, h * HEAD_PAD + LANES:(h + 1) * HEAD_PAD] = krope
    vt = kv[:, nope_w:].T
    vt_ref[:, 0] = vt.reshape(MLA_HEADS, MLA_V, vt.shape[-1]).astype(BF16)


def _dq_store(xb, w_ref, tab_ref, qt_ref, scale):
    dq = jnp.dot(xb, w_ref[...], preferred_element_type=F32)
    first_map = lax.broadcasted_iota(jnp.int32, (dq.shape[0], LANES), 1) < DIFF_QK
    for h in range(DIFF_HEADS):
        r = _rope_mix(dq[:, h * LANES:(h + 1) * LANES], tab_ref, DIFF_ROT // 2) * scale
        r1 = jnp.where(first_map, r, 0.0).T.astype(BF16)
        r2 = jnp.where(first_map, 0.0, r).T.astype(BF16)
        for blk in range(qt_ref.shape[1]):
            cols = slice(blk * DIFF_BQ, (blk + 1) * DIFF_BQ)
            qt_ref[h, blk, :, :DIFF_BQ] = r1[:, cols]
            qt_ref[h, blk, :, DIFF_BQ:] = r2[:, cols]


def _dk_store(xb, w_ref, tab_ref, k_ref):
    dk = jnp.dot(xb, w_ref[...], preferred_element_type=F32)
    for h in range(DIFF_HEADS):
        k_ref[:, h * LANES:(h + 1) * LANES] = _rope_mix(
            dk[:, h * LANES:(h + 1) * LANES], tab_ref, DIFF_ROT // 2).astype(BF16)


def _dv_store(xb, w_ref, vt_ref):
    dv = jnp.dot(xb, w_ref[...], preferred_element_type=F32)
    vt = dv.T
    vt_ref[:, 0] = vt.reshape(DIFF_HEADS, DIFF_V, vt.shape[-1]).astype(BF16)


def _diff_proj_kernel(x_ref, wq_ref, wk_ref, wv_ref, tab_ref, qt_ref, k_ref, vt_ref, *, scale):
    xb = x_ref[...].astype(BF16)
    _dq_store(xb, wq_ref, tab_ref, qt_ref, scale)
    _dk_store(xb, wk_ref, tab_ref, k_ref)
    _dv_store(xb, wv_ref, vt_ref)


def _gate_kernel(x_ref, w_ref, b_ref, g_ref):
    xb = x_ref[...].astype(BF16)
    z = jnp.dot(xb, w_ref[...], preferred_element_type=F32) + b_ref[...]
    g_ref[...] = (1.0 / (1.0 + jnp.exp(-z))).astype(g_ref.dtype)


def _row_spec(tm, width):
    return pl.BlockSpec((tm, width), lambda i: (i, 0))


def _full_spec(shape):
    nd = len(shape)
    return pl.BlockSpec(shape, lambda i: (0,) * nd, pipeline_mode=pl.Buffered(1))


def _tab_spec(tm, s_tiles):
    return pl.BlockSpec((3, tm, LANES), lambda i: (0, i % s_tiles, 0))


def _vt_out(t, tm, key_chunk):
    per_chunk = key_chunk // tm
    shape = jax.ShapeDtypeStruct((MLA_HEADS, t // key_chunk, MLA_V, key_chunk), BF16)
    spec = pl.BlockSpec((MLA_HEADS, 1, MLA_V, tm), lambda i: (0, i // per_chunk, 0, i % per_chunk))
    return shape, spec


def _q_proj(x, wqa, gqa, wqb, tab, seq):
    t, tm = x.shape[0], PROJ_TM
    blocks = tm // MLA_BQ
    return pl.pallas_call(
        functools.partial(_q_proj_kernel, scale=LOG2_E * (MLA_NOPE + MLA_ROPE) ** -0.5),
        grid=(t // tm,),
        in_specs=[_row_spec(tm, D_MODEL), _full_spec(wqa.shape), _full_spec(gqa.shape),
                  _full_spec(wqb.shape), _tab_spec(tm, seq // tm)],
        out_specs=pl.BlockSpec((MLA_HEADS, blocks, HEAD_PAD, MLA_BQ), lambda i: (0, i, 0, 0)),
        out_shape=jax.ShapeDtypeStruct((MLA_HEADS, t // MLA_BQ, HEAD_PAD, MLA_BQ), BF16),
        compiler_params=_params(("parallel",)),
        name="mla_q_proj",
    )(x, wqa, gqa, wqb, tab)


def _kv_proj(x, wkva, gkva, wkvb, tab, seq):
    t, tm = x.shape[0], VT_TM
    vt_shape, vt_spec = _vt_out(t, tm, MLA_KEY_CHUNK)
    return pl.pallas_call(
        _kv_proj_kernel,
        grid=(t // tm,),
        in_specs=[_row_spec(tm, D_MODEL), _full_spec(wkva.shape), _full_spec(gkva.shape),
                  _full_spec(wkvb.shape), _tab_spec(tm, seq // tm)],
        out_specs=[_row_spec(tm, MLA_HEADS * HEAD_PAD), vt_spec],
        out_shape=[jax.ShapeDtypeStruct((t, MLA_HEADS * HEAD_PAD), BF16), vt_shape],
        compiler_params=_params(("parallel",)),
        name="mla_kv_proj",
    )(x, wkva, gkva, wkvb, tab)


def _diff_proj(x, wq, wk, wv, tab, seq):
    t, tm = x.shape[0], MIX_TM
    blocks = tm // DIFF_BQ
    vt_shape, vt_spec = _vt_out(t, tm, DIFF_KEY_CHUNK)
    return pl.pallas_call(
        functools.partial(_diff_proj_kernel, scale=LOG2_E * DIFF_QK ** -0.5),
        grid=(t // tm,),
        in_specs=[_row_spec(tm, D_MODEL), _full_spec(wq.shape), _full_spec(wk.shape), _full_spec(wv.shape),
                  _tab_spec(tm, seq // tm)],
        out_specs=[pl.BlockSpec((DIFF_HEADS, blocks, LANES, 2 * DIFF_BQ), lambda i: (0, i, 0, 0)),
                   _row_spec(tm, C_DK), vt_spec],
        out_shape=[jax.ShapeDtypeStruct((DIFF_HEADS, t // DIFF_BQ, LANES, 2 * DIFF_BQ), BF16),
                   jax.ShapeDtypeStruct((t, C_DK), BF16), vt_shape],
        compiler_params=_params(("parallel",)),
        name="diff_proj",
    )(x, wq, wk, wv, tab)


def _gates(x, w, b):
    t, tm = x.shape[0], PROJ_TM
    n = w.shape[1]
    tn = D_MODEL
    return pl.pallas_call(
        _gate_kernel,
        grid=(n // tn, t // tm),
        in_specs=[pl.BlockSpec((tm, D_MODEL), lambda j, i: (i, 0)),
                  pl.BlockSpec((D_MODEL, tn), lambda j, i: (0, j), pipeline_mode=pl.Buffered(1)),
                  pl.BlockSpec((1, tn), lambda j, i: (0, j))],
        out_specs=pl.BlockSpec((tm, tn), lambda j, i: (i, j)),
        out_shape=jax.ShapeDtypeStruct((t, n), BF16),
        compiler_params=_params(("parallel", "parallel")),
        name="gates",
    )(x, w, b)


def _flash_head(qt_ref, k_ref, vt_ref, s_ref, p_ref, m_ref, l_ref, acc_ref):
    nq, _, nc = qt_ref.shape
    n_chunks, _, key_chunk = vt_ref.shape
    n_steps = n_chunks * nq
    n_tiles = key_chunk // KEY_TILE
    block_bits = nq.bit_length() - 1
    assert nq == 1 << block_bits and n_steps % 2 == 0

    def chunk_and_block(g):
        return lax.shift_right_logical(g, block_bits), lax.bitwise_and(g, nq - 1)

    def sublane_groups(x):
        return x.reshape(KEY_TILE // SUBLANES, SUBLANES, nc)

    def score_tile(g, t):
        chunk, blk = chunk_and_block(g)
        start = pl.multiple_of(chunk * key_chunk + t * KEY_TILE, KEY_TILE)
        return jnp.dot(k_ref[pl.ds(start, KEY_TILE), :], qt_ref[blk], preferred_element_type=F32)

    def add_pv(g_prev, alpha_prev):
        chunk, blk = chunk_and_block(g_prev)
        pv = None
        for h in range(key_chunk // PV_SPAN):
            span = slice(h * PV_SPAN, (h + 1) * PV_SPAN)
            part = jnp.dot(vt_ref[chunk, :, span], p_ref[span, :], preferred_element_type=F32)
            pv = part if h == 0 else pv + part
        acc_ref[blk] = alpha_prev * acc_ref[blk] + pv

    def step(g, carry):
        alpha_prev, s_max = carry
        _, blk = chunk_and_block(g)
        m = m_ref[blk]
        m_new = jnp.maximum(m, s_max)
        alpha = jnp.exp2(m - m_new)
        m_ref[blk] = m_new
        add_pv(jnp.maximum(g - 1, 0), alpha_prev)
        g_next = jnp.minimum(g + 1, n_steps - 1)
        next_max = None
        p_sum = None
        for t in range(n_tiles):
            rows = pl.ds(t * KEY_TILE, KEY_TILE)
            p = jnp.exp2(s_ref[rows, :] - m_new)
            p_ref[rows, :] = p.astype(BF16)
            tile_sum = jnp.sum(sublane_groups(p), axis=0)
            p_sum = tile_sum if t == 0 else p_sum + tile_sum
            s_tile = score_tile(g_next, t)
            s_ref[rows, :] = s_tile
            tile_max = jnp.max(sublane_groups(s_tile), axis=0)
            next_max = tile_max if t == 0 else jnp.maximum(next_max, tile_max)
        l_ref[blk] = alpha * l_ref[blk] + jnp.sum(p_sum, axis=0, keepdims=True)
        return alpha, jnp.max(next_max, axis=0, keepdims=True)

    first_max = None
    for t in range(n_tiles):
        s_tile = score_tile(0, t)
        s_ref[pl.ds(t * KEY_TILE, KEY_TILE), :] = s_tile
        tile_max = jnp.max(sublane_groups(s_tile), axis=0)
        first_max = tile_max if t == 0 else jnp.maximum(first_max, tile_max)
    p_ref[...] = jnp.zeros(p_ref.shape, BF16)
    m_ref[...] = jnp.full(m_ref.shape, -jnp.inf, F32)
    l_ref[...] = jnp.zeros(l_ref.shape, F32)
    acc_ref[...] = jnp.zeros(acc_ref.shape, F32)

    init = (jnp.ones((1, nc), F32), jnp.max(first_max, axis=0, keepdims=True))
    alpha_last, _ = lax.fori_loop(0, n_steps, step, init, unroll=2)
    add_pv(n_steps - 1, alpha_last)


def _mla_flash_kernel(qt_ref, k_ref, vt_ref, o_ref, s_ref, p_ref, m_ref, l_ref, acc_ref):
    _flash_head(qt_ref, k_ref, vt_ref, s_ref, p_ref, m_ref, l_ref, acc_ref)
    nq, _, nc = qt_ref.shape

    def finish(blk, _):
        o = acc_ref[blk] * (1.0 / l_ref[blk])
        o_ref[pl.ds(pl.multiple_of(blk * nc, nc), nc), :] = o.T.astype(BF16)
        return 0

    lax.fori_loop(0, nq, finish, 0, unroll=8)


def _diff_flash_kernel(qt_ref, k_ref, vt_ref, lq_ref, lk_ref, gsub_ref, o_ref,
                       s_ref, p_ref, m_ref, l_ref, acc_ref, *, lam_init):
    _flash_head(qt_ref, k_ref, vt_ref, s_ref, p_ref, m_ref, l_ref, acc_ref)
    nq, _, nc = qt_ref.shape
    bq = nc // 2
    lam_dot = jnp.sum(lq_ref[...] * lk_ref[...], axis=-1, keepdims=True)
    lam_exp = jnp.exp(lam_dot)
    lam = lam_exp[0:1, :] - lam_exp[1:2, :] + lam_init

    def finish(blk, _):
        on = acc_ref[blk] * (1.0 / l_ref[blk])
        a = on[:, :bq] - lam * on[:, bq:]
        ms = jnp.mean(a * a, axis=0, keepdims=True)
        y = (a * lax.rsqrt(ms + RMS_EPS)).T * gsub_ref[...]
        o_ref[pl.ds(pl.multiple_of(blk * bq, bq), bq), :] = (y * (1.0 - lam_init)).astype(BF16)
        return 0

    lax.fori_loop(0, nq, finish, 0, unroll=8)


def _flash_call(kernel, qt, k, vt, extra, batch, seq, dk, dv, nc, key_chunk, name):
    heads = qt.shape[0]
    nq = qt.shape[1] // batch
    n_chunks = seq // key_chunk
    in_bytes = 2 * (nq * dk * nc + seq * dk + seq * dv)
    other_bytes = key_chunk * nc * (4 + 2) + nq * (dv + 2) * nc * 4 + 2 * seq * dv * 2
    once = None if 2 * in_bytes + other_bytes <= VMEM_LIMIT else pl.Buffered(1)
    small = lambda shape: pl.BlockSpec(shape, lambda b, h: (0, 0))
    return pl.pallas_call(
        kernel,
        grid=(batch, heads),
        in_specs=[pl.BlockSpec((None, nq, dk, nc), lambda b, h: (h, b, 0, 0), pipeline_mode=once),
                  pl.BlockSpec((seq, dk), lambda b, h: (b, h), pipeline_mode=once),
                  pl.BlockSpec((None, n_chunks, dv, key_chunk), lambda b, h: (h, b, 0, 0),
                               pipeline_mode=once)] + [small(e.shape) for e in extra],
        out_specs=pl.BlockSpec((seq, dv), lambda b, h: (b, h)),
        out_shape=jax.ShapeDtypeStruct((batch * seq, heads * dv), BF16),
        scratch_shapes=[pltpu.VMEM((key_chunk, nc), F32), pltpu.VMEM((key_chunk, nc), BF16),
                        pltpu.VMEM((nq, 1, nc), F32), pltpu.VMEM((nq, 1, nc), F32),
                        pltpu.VMEM((nq, dv, nc), F32)],
        compiler_params=_params(("parallel", "parallel")),
        name=name,
    )(qt, k, vt, *extra)


def _mla_flash(qt, k, vt, batch, seq):
    return _flash_call(_mla_flash_kernel, qt, k, vt, (), batch, seq, HEAD_PAD, MLA_V, MLA_BQ,
                       MLA_KEY_CHUNK, "mla_flash")


def _diff_flash(qt, k, vt, lam_q, lam_k, g_sub, batch, seq, lam_init):
    return _flash_call(functools.partial(_diff_flash_kernel, lam_init=lam_init), qt, k, vt,
                       (lam_q, lam_k, g_sub), batch, seq, LANES, DIFF_V, 2 * DIFF_BQ, DIFF_KEY_CHUNK, "diff_flash")


def _merge_kernel(om_ref, od_ref, gm_ref, gd_ref, wm_ref, wd_ref, o_ref):
    a = jnp.dot(om_ref[...], wm_ref[...], preferred_element_type=F32)
    b = jnp.dot(od_ref[...], wd_ref[...], preferred_element_type=F32)
    o_ref[...] = (gm_ref[...] * a + gd_ref[...] * b).astype(BF16)


def _outproj_ln_kernel(x_ref, m_ref, w_ref, g_ref, b_ref, o_ref):
    h = jnp.dot(m_ref[...], w_ref[...], preferred_element_type=F32)
    o_ref[...] = _layernorm_rows(DN_ALPHA * x_ref[...] + h, g_ref[...], b_ref[...])


def _ffn_ln_kernel(x_ref, w1_ref, w2_ref, g_ref, b_ref, o_ref, xb_ref, acc_ref):
    j = pl.program_id(1)

    @pl.when(j == 0)
    def _():
        xb_ref[...] = x_ref[...].astype(BF16)
        acc_ref[...] = jnp.zeros_like(acc_ref)

    h = jnp.maximum(jnp.dot(xb_ref[...], w1_ref[...], preferred_element_type=F32), 0.0)
    acc_ref[...] += jnp.dot((h * h).astype(BF16), w2_ref[...], preferred_element_type=F32)

    @pl.when(j == pl.num_programs(1) - 1)
    def _():
        o_ref[...] = _layernorm_rows(DN_ALPHA * x_ref[...] + acc_ref[...], g_ref[...], b_ref[...])


def _merge(o_mla, o_diff, gates, w_br_mla, w_br_diff):
    t, tm = o_mla.shape[0], MIX_TM
    return pl.pallas_call(
        _merge_kernel,
        grid=(t // tm,),
        in_specs=[_row_spec(tm, o_mla.shape[1]), _row_spec(tm, o_diff.shape[1]),
                  pl.BlockSpec((tm, D_MODEL), lambda i: (i, 0)),
                  pl.BlockSpec((tm, D_MODEL), lambda i: (i, 1)),
                  _full_spec(w_br_mla.shape), _full_spec(w_br_diff.shape)],
        out_specs=_row_spec(tm, D_MODEL),
        out_shape=jax.ShapeDtypeStruct((t, D_MODEL), BF16),
        compiler_params=_params(("parallel",)),
        name="branch_merge",
    )(o_mla, o_diff, gates, gates, w_br_mla, w_br_diff)


def _outproj_ln(x, merged, w_out, g, b):
    t, tm = x.shape[0], MIX_TM
    return pl.pallas_call(
        _outproj_ln_kernel,
        grid=(t // tm,),
        in_specs=[_row_spec(tm, D_MODEL), _row_spec(tm, D_MODEL), _full_spec(w_out.shape),
                  _full_spec(g.shape), _full_spec(b.shape)],
        out_specs=_row_spec(tm, D_MODEL),
        out_shape=jax.ShapeDtypeStruct((t, D_MODEL), F32),
        compiler_params=_params(("parallel",)),
        name="outproj_ln",
    )(x, merged, w_out, g, b)


def _ffn_ln(x, w1, w2, g, b):
    t, tm, tf = x.shape[0], MIX_TM, FFN_TF
    return pl.pallas_call(
        _ffn_ln_kernel,
        grid=(t // tm, D_FF // tf),
        in_specs=[pl.BlockSpec((tm, D_MODEL), lambda i, j: (i, 0)),
                  pl.BlockSpec((D_MODEL, tf), lambda i, j: (0, j)),
                  pl.BlockSpec((tf, D_MODEL), lambda i, j: (j, 0)),
                  pl.BlockSpec((1, D_MODEL), lambda i, j: (0, 0)),
                  pl.BlockSpec((1, D_MODEL), lambda i, j: (0, 0))],
        out_specs=pl.BlockSpec((tm, D_MODEL), lambda i, j: (i, 0)),
        out_shape=jax.ShapeDtypeStruct((t, D_MODEL), F32),
        scratch_shapes=[pltpu.VMEM((tm, D_MODEL), BF16), pltpu.VMEM((tm, D_MODEL), F32)],
        compiler_params=_params(("parallel", "arbitrary")),
        name="ffn_ln",
    )(x, w1, w2, g, b)


def _rope_tables(seq, rot_dim, group):
    half = rot_dim // 2
    inv_freq = ROPE_THETA ** (-jnp.arange(0, rot_dim, 2, dtype=F32) / rot_dim)
    ang = jnp.arange(seq, dtype=F32)[:, None] * inv_freq[None, :]
    cos, sin = jnp.cos(ang), jnp.sin(ang)
    zeros = lambda n: jnp.zeros((seq, n), F32)
    c = jnp.concatenate([cos, cos, jnp.ones((seq, group - rot_dim), F32)], axis=1)
    s_fwd = jnp.concatenate([-sin, zeros(group - half)], axis=1)
    s_bwd = jnp.concatenate([zeros(half), sin, zeros(group - rot_dim)], axis=1)
    return jnp.stack([c, s_fwd, s_bwd])


def _prep_weights(w_in, b_gate, g_qa, w_qb, g_kva, w_kvb, g_sub, w_br_mla, w_br_diff, w_out,
                  ln1_g, ln1_b, w_ff1, w_ff2, ln2_g, ln2_b, l):
    c0 = C_QA
    c1 = c0 + C_KVA
    c2 = c1 + C_DQ
    c3 = c2 + C_DK
    c4 = c3 + C_DV
    wi = w_in[l]
    qb = w_qb[l].reshape(MLA_Q_LORA, MLA_HEADS, MLA_NOPE + MLA_ROPE)
    qb_nope = qb[:, :, :MLA_NOPE].reshape(MLA_Q_LORA, MLA_HEADS * MLA_NOPE)
    qb_rope = jnp.pad(qb[:, :, MLA_NOPE:], ((0, 0), (0, 0), (0, LANES - MLA_ROPE)))
    qb_rope = qb_rope.reshape(MLA_Q_LORA, MLA_HEADS * LANES)
    kvb = w_kvb[l].reshape(MLA_KV_LORA, MLA_HEADS, MLA_NOPE + MLA_V)
    kvb_k = kvb[:, :, :MLA_NOPE].reshape(MLA_KV_LORA, MLA_HEADS * MLA_NOPE)
    kvb_v = kvb[:, :, MLA_NOPE:].reshape(MLA_KV_LORA, MLA_HEADS * MLA_V)
    row = lambda v: v[l].reshape(1, -1)
    return dict(
        wqa=wi[:, :c0].astype(BF16),
        wkva=jnp.pad(wi[:, c0:c1], ((0, 0), (0, LANES - MLA_ROPE))).astype(BF16),
        wdq=wi[:, c1:c2].astype(BF16),
        wdk=wi[:, c2:c3].astype(BF16),
        wdv=wi[:, c3:c4].astype(BF16),
        wgate=wi[:, c4:].astype(BF16),
        b_gate=row(b_gate),
        g_qa=row(g_qa),
        wqb=jnp.concatenate([qb_nope, qb_rope], axis=1).astype(BF16),
        g_kva=row(g_kva),
        wkvb=jnp.concatenate([kvb_k, kvb_v], axis=1).astype(BF16),
        g_sub=row(g_sub),
        w_br_mla=w_br_mla[l].astype(BF16),
        w_br_diff=w_br_diff[l].astype(BF16),
        w_out=w_out[l].astype(BF16),
        ln1_g=row(ln1_g), ln1_b=row(ln1_b),
        w_ff1=w_ff1[l].astype(BF16), w_ff2=w_ff2[l].astype(BF16),
        ln2_g=row(ln2_g), ln2_b=row(ln2_b),
    )


def _layer(x3, w, tab_mla, tab_diff, lam_q, lam_k, lam_init):
    batch, seq, _ = x3.shape
    x = x3.reshape(batch * seq, D_MODEL)

    qt = _q_proj(x, w["wqa"], w["g_qa"], w["wqb"], tab_mla, seq)
    k, vt = _kv_proj(x, w["wkva"], w["g_kva"], w["wkvb"], tab_mla, seq)
    o_mla = _mla_flash(qt, k, vt, batch, seq)

    dqt, dk, dvt = _diff_proj(x, w["wdq"], w["wdk"], w["wdv"], tab_diff, seq)
    o_diff = _diff_flash(dqt, dk, dvt, lam_q, lam_k, w["g_sub"], batch, seq, lam_init)

    gates = _gates(x, w["wgate"], w["b_gate"])
    merged = _merge(o_mla, o_diff, gates, w["w_br_mla"], w["w_br_diff"])
    x1 = _outproj_ln(x, merged, w["w_out"], w["ln1_g"], w["ln1_b"])
    y = _ffn_ln(x1, w["w_ff1"], w["w_ff2"], w["ln2_g"], w["ln2_b"])
    return y.reshape(batch, seq, D_MODEL)


def kernel(x_prompt, x_sample, w_in, b_gate, g_qa, w_qb, g_kva, w_kvb, lam_q, lam_k, g_sub,
           w_br_mla, w_br_diff, w_out, ln1_g, ln1_b, w_ff1, w_ff2, ln2_g, ln2_b):
    outs = [x_prompt, x_sample]
    max_seq = max(x3.shape[1] for x3 in outs)
    tab_mla = jnp.pad(_rope_tables(max_seq, MLA_ROPE, MLA_ROPE), ((0, 0), (0, 0), (0, LANES - MLA_ROPE)))
    tab_diff = jnp.tile(_rope_tables(max_seq, DIFF_ROT, DIFF_QK), (1, 1, LANES // DIFF_QK))
    for l in range(DEPTH):
        lam_init = 0.8 - 0.6 * math.exp(-0.3 * l)
        w = _prep_weights(w_in, b_gate, g_qa, w_qb, g_kva, w_kvb, g_sub, w_br_mla, w_br_diff, w_out,
                          ln1_g, ln1_b, w_ff1, w_ff2, ln2_g, ln2_b, l)
        outs = [_layer(x3, w, tab_mla, tab_diff, lam_q[l], lam_k[l], lam_init) for x3 in outs]
    return tuple(outs)
```

```python
import functools
import math

import jax
import jax.numpy as jnp
from jax import lax
from jax.experimental import pallas as pl
from jax.experimental.pallas import tpu as pltpu

D_MODEL = 2048
DEPTH = 1
MLA_HEADS = 8
MLA_Q_LORA = 768
MLA_KV_LORA = 512
MLA_NOPE = 128
MLA_ROPE = 64
MLA_V = 128
DIFF_HEADS = 8
DIFF_QK = 64
DIFF_V = 2 * DIFF_QK
DIFF_ROT = DIFF_QK // 4
D_FF = 4 * D_MODEL
ROPE_THETA = 500000.0
LN_EPS = 1e-5
RMS_EPS = 1e-6
DN_ALPHA = (2.0 * DEPTH) ** 0.25
LOG2_E = math.log2(math.e)

C_QA = MLA_Q_LORA
C_KVA = MLA_KV_LORA + MLA_ROPE
C_DQ = DIFF_HEADS * 2 * DIFF_QK
C_DK = DIFF_HEADS * 2 * DIFF_QK
C_DV = DIFF_HEADS * DIFF_V

LANES = 128
HEAD_PAD = 2 * LANES
MLA_KEY_CHUNK = 2048
DIFF_KEY_CHUNK = 2048
KEY_TILE = 1024
PV_SPAN = 1024
SUBLANES = 8
VT_TM = 512
PROJ_TM = 1024
MIX_TM = 512
FFN_TF = 1024
MLA_BQ = 512
DIFF_BQ = 256
VMEM_LIMIT = 56 * 1024 * 1024

F32 = jnp.float32
BF16 = jnp.bfloat16


def _params(sem):
    return pltpu.CompilerParams(dimension_semantics=sem, vmem_limit_bytes=VMEM_LIMIT)


def _rope_mix(x, tab_ref, shift):
    n = x.shape[-1]
    fwd = pltpu.roll(x, n - shift, 1)
    bwd = pltpu.roll(x, shift, 1)
    return x * tab_ref[0] + fwd * tab_ref[1] + bwd * tab_ref[2]


def _rmsnorm_rows(x, g):
    ms = jnp.mean(x * x, axis=-1, keepdims=True)
    return x * lax.rsqrt(ms + RMS_EPS) * g


def _layernorm_rows(z, g, b):
    mu = jnp.mean(z, axis=-1, keepdims=True)
    zc = z - mu
    var = jnp.mean(zc * zc, axis=-1, keepdims=True)
    return zc * lax.rsqrt(var + LN_EPS) * g + b


def _q_store(xb, wqa_ref, gqa_ref, wqb_ref, tab_ref, qt_ref, scale):
    qa = jnp.dot(xb, wqa_ref[...], preferred_element_type=F32)
    qn = _rmsnorm_rows(qa, gqa_ref[...]).astype(BF16)
    q = jnp.dot(qn, wqb_ref[...], preferred_element_type=F32)
    nope_w = MLA_HEADS * MLA_NOPE
    for h in range(MLA_HEADS):
        nope = q[:, h * MLA_NOPE:(h + 1) * MLA_NOPE]
        rope = _rope_mix(q[:, nope_w + h * LANES: nope_w + (h + 1) * LANES], tab_ref, MLA_ROPE // 2)
        nope_t = (nope * scale).T.astype(BF16)
        rope_t = (rope * scale).T.astype(BF16)
        for blk in range(qt_ref.shape[1]):
            cols = slice(blk * MLA_BQ, (blk + 1) * MLA_BQ)
            qt_ref[h, blk, :LANES, :] = nope_t[:, cols]
            qt_ref[h, blk, LANES:, :] = rope_t[:, cols]


def _kv_store(xb, wkva_ref, gkva_ref, wkvb_ref, tab_ref, k_ref, vt_ref):
    kva = jnp.dot(xb, wkva_ref[...], preferred_element_type=F32)
    ckv = _rmsnorm_rows(kva[:, :MLA_KV_LORA], gkva_ref[...]).astype(BF16)
    krope = _rope_mix(kva[:, MLA_KV_LORA:], tab_ref, MLA_ROPE // 2).astype(BF16)
    kv = jnp.dot(ckv, wkvb_ref[...], preferred_element_type=F32)
    nope_w = MLA_HEADS * MLA_NOPE
    for h in range(MLA_HEADS):
        k_ref[:, h * HEAD_PAD: h * HEAD_PAD + LANES] = kv[:, h * MLA_NOPE:(h + 1) * MLA_NOPE].astype(BF16)
        k_ref[:, h * HEAD_PAD + LANES:(h + 1) * HEAD_PAD] = krope
    vt = kv[:, nope_w:].T
    vt_ref[:, 0] = vt.reshape(MLA_HEADS, MLA_V, vt.shape[-1]).astype(BF16)


def _mla_proj_kernel(x_ref, wqa_ref, gqa_ref, wqb_ref, wkva_ref, gkva_ref, wkvb_ref, tab_ref,
                     qt_ref, k_ref, vt_ref, *, scale):
    xb = x_ref[...].astype(BF16)
    _q_store(xb, wqa_ref, gqa_ref, wqb_ref, tab_ref, qt_ref, scale)
    _kv_store(xb, wkva_ref, gkva_ref, wkvb_ref, tab_ref, k_ref, vt_ref)


def _dq_store(xb, w_ref, tab_ref, qt_ref, scale):
    dq = jnp.dot(xb, w_ref[...], preferred_element_type=F32)
    first_map = lax.broadcasted_iota(jnp.int32, (dq.shape[0], LANES), 1) < DIFF_QK
    for h in range(DIFF_HEADS):
        r = _rope_mix(dq[:, h * LANES:(h + 1) * LANES], tab_ref, DIFF_ROT // 2) * scale
        r1 = jnp.where(first_map, r, 0.0).T.astype(BF16)
        r2 = jnp.where(first_map, 0.0, r).T.astype(BF16)
        for blk in range(qt_ref.shape[1]):
            cols = slice(blk * DIFF_BQ, (blk + 1) * DIFF_BQ)
            qt_ref[h, blk, :, :DIFF_BQ] = r1[:, cols]
            qt_ref[h, blk, :, DIFF_BQ:] = r2[:, cols]


def _dk_store(xb, w_ref, tab_ref, k_ref):
    dk = jnp.dot(xb, w_ref[...], preferred_element_type=F32)
    for h in range(DIFF_HEADS):
        k_ref[:, h * LANES:(h + 1) * LANES] = _rope_mix(
            dk[:, h * LANES:(h + 1) * LANES], tab_ref, DIFF_ROT // 2).astype(BF16)


def _dv_store(xb, w_ref, vt_ref):
    dv = jnp.dot(xb, w_ref[...], preferred_element_type=F32)
    vt = dv.T
    vt_ref[:, 0] = vt.reshape(DIFF_HEADS, DIFF_V, vt.shape[-1]).astype(BF16)


def _diff_proj_kernel(x_ref, wq_ref, wk_ref, wv_ref, tab_ref, qt_ref, k_ref, vt_ref, *, scale):
    xb = x_ref[...].astype(BF16)
    _dq_store(xb, wq_ref, tab_ref, qt_ref, scale)
    _dk_store(xb, wk_ref, tab_ref, k_ref)
    _dv_store(xb, wv_ref, vt_ref)


def _gate_kernel(x_ref, w_ref, b_ref, g_ref):
    xb = x_ref[...].astype(BF16)
    z = jnp.dot(xb, w_ref[...], preferred_element_type=F32) + b_ref[...]
    g_ref[...] = (1.0 / (1.0 + jnp.exp(-z))).astype(g_ref.dtype)


def _row_spec(tm, width):
    return pl.BlockSpec((tm, width), lambda i: (i, 0))


def _full_spec(shape):
    nd = len(shape)
    return pl.BlockSpec(shape, lambda i: (0,) * nd, pipeline_mode=pl.Buffered(1))


def _tab_spec(tm, s_tiles):
    return pl.BlockSpec((3, tm, LANES), lambda i: (0, i % s_tiles, 0))


def _vt_out(t, tm, key_chunk):
    per_chunk = key_chunk // tm
    shape = jax.ShapeDtypeStruct((MLA_HEADS, t // key_chunk, MLA_V, key_chunk), BF16)
    spec = pl.BlockSpec((MLA_HEADS, 1, MLA_V, tm), lambda i: (0, i // per_chunk, 0, i % per_chunk))
    return shape, spec


def _mla_proj(x, wqa, gqa, wqb, wkva, gkva, wkvb, tab, seq):
    t, tm = x.shape[0], VT_TM
    blocks = tm // MLA_BQ
    vt_shape, vt_spec = _vt_out(t, tm, MLA_KEY_CHUNK)
    weights = (wqa, gqa, wqb, wkva, gkva, wkvb)
    return pl.pallas_call(
        functools.partial(_mla_proj_kernel, scale=LOG2_E * (MLA_NOPE + MLA_ROPE) ** -0.5),
        grid=(t // tm,),
        in_specs=[_row_spec(tm, D_MODEL)] + [_full_spec(w.shape) for w in weights] + [_tab_spec(tm, seq // tm)],
        out_specs=[pl.BlockSpec((MLA_HEADS, blocks, HEAD_PAD, MLA_BQ), lambda i: (0, i, 0, 0)),
                   _row_spec(tm, MLA_HEADS * HEAD_PAD), vt_spec],
        out_shape=[jax.ShapeDtypeStruct((MLA_HEADS, t // MLA_BQ, HEAD_PAD, MLA_BQ), BF16),
                   jax.ShapeDtypeStruct((t, MLA_HEADS * HEAD_PAD), BF16), vt_shape],
        compiler_params=_params(("parallel",)),
        name="mla_proj",
    )(x, *weights, tab)


def _diff_proj(x, wq, wk, wv, tab, seq):
    t, tm = x.shape[0], MIX_TM
    blocks = tm // DIFF_BQ
    vt_shape, vt_spec = _vt_out(t, tm, DIFF_KEY_CHUNK)
    return pl.pallas_call(
        functools.partial(_diff_proj_kernel, scale=LOG2_E * DIFF_QK ** -0.5),
        grid=(t // tm,),
        in_specs=[_row_spec(tm, D_MODEL), _full_spec(wq.shape), _full_spec(wk.shape), _full_spec(wv.shape),
                  _tab_spec(tm, seq // tm)],
        out_specs=[pl.BlockSpec((DIFF_HEADS, blocks, LANES, 2 * DIFF_BQ), lambda i: (0, i, 0, 0)),
                   _row_spec(tm, C_DK), vt_spec],
        out_shape=[jax.ShapeDtypeStruct((DIFF_HEADS, t // DIFF_BQ, LANES, 2 * DIFF_BQ), BF16),
                   jax.ShapeDtypeStruct((t, C_DK), BF16), vt_shape],
        compiler_params=_params(("parallel",)),
        name="diff_proj",
    )(x, wq, wk, wv, tab)


def _gates(x, w, b):
    t, tm = x.shape[0], PROJ_TM
    n = w.shape[1]
    tn = D_MODEL
    return pl.pallas_call(
        _gate_kernel,
        grid=(n // tn, t // tm),
        in_specs=[pl.BlockSpec((tm, D_MODEL), lambda j, i: (i, 0)),
                  pl.BlockSpec((D_MODEL, tn), lambda j, i: (0, j), pipeline_mode=pl.Buffered(1)),
                  pl.BlockSpec((1, tn), lambda j, i: (0, j))],
        out_specs=pl.BlockSpec((tm, tn), lambda j, i: (i, j)),
        out_shape=jax.ShapeDtypeStruct((t, n), BF16),
        compiler_params=_params(("parallel", "parallel")),
        name="gates",
    )(x, w, b)


def _flash_head(qt_ref, k_ref, vt_ref, s_ref, p_ref, m_ref, l_ref, acc_ref):
    nq, _, nc = qt_ref.shape
    n_chunks, _, key_chunk = vt_ref.shape
    n_steps = n_chunks * nq
    n_tiles = key_chunk // KEY_TILE
    block_bits = nq.bit_length() - 1
    assert nq == 1 << block_bits and n_steps % 2 == 0

    def chunk_and_block(g):
        return lax.shift_right_logical(g, block_bits), lax.bitwise_and(g, nq - 1)

    def sublane_groups(x):
        return x.reshape(KEY_TILE // SUBLANES, SUBLANES, nc)

    def score_tile(g, t):
        chunk, blk = chunk_and_block(g)
        start = pl.multiple_of(chunk * key_chunk + t * KEY_TILE, KEY_TILE)
        return jnp.dot(k_ref[pl.ds(start, KEY_TILE), :], qt_ref[blk], preferred_element_type=F32)

    def add_pv(g_prev, alpha_prev):
        chunk, blk = chunk_and_block(g_prev)
        pv = None
        for h in range(key_chunk // PV_SPAN):
            span = slice(h * PV_SPAN, (h + 1) * PV_SPAN)
            part = jnp.dot(vt_ref[chunk, :, span], p_ref[span, :], preferred_element_type=F32)
            pv = part if h == 0 else pv + part
        acc_ref[blk] = alpha_prev * acc_ref[blk] + pv

    def step(g, carry):
        alpha_prev, s_max = carry
        _, blk = chunk_and_block(g)
        m = m_ref[blk]
        m_new = jnp.maximum(m, s_max)
        alpha = jnp.exp2(m - m_new)
        m_ref[blk] = m_new
        add_pv(jnp.maximum(g - 1, 0), alpha_prev)
        g_next = jnp.minimum(g + 1, n_steps - 1)
        next_max = None
        p_sum = None
        for t in range(n_tiles):
            rows = pl.ds(t * KEY_TILE, KEY_TILE)
            p = jnp.exp2(s_ref[rows, :] - m_new)
            p_ref[rows, :] = p.astype(BF16)
            tile_sum = jnp.sum(sublane_groups(p), axis=0)
            p_sum = tile_sum if t == 0 else p_sum + tile_sum
            s_tile = score_tile(g_next, t)
            s_ref[rows, :] = s_tile
            tile_max = jnp.max(sublane_groups(s_tile), axis=0)
            next_max = tile_max if t == 0 else jnp.maximum(next_max, tile_max)
        l_ref[blk] = alpha * l_ref[blk] + jnp.sum(p_sum, axis=0, keepdims=True)
        return alpha, jnp.max(next_max, axis=0, keepdims=True)

    first_max = None
    for t in range(n_tiles):
        s_tile = score_tile(0, t)
        s_ref[pl.ds(t * KEY_TILE, KEY_TILE), :] = s_tile
        tile_max = jnp.max(sublane_groups(s_tile), axis=0)
        first_max = tile_max if t == 0 else jnp.maximum(first_max, tile_max)
    p_ref[...] = jnp.zeros(p_ref.shape, BF16)
    m_ref[...] = jnp.full(m_ref.shape, -jnp.inf, F32)
    l_ref[...] = jnp.zeros(l_ref.shape, F32)
    acc_ref[...] = jnp.zeros(acc_ref.shape, F32)

    init = (jnp.ones((1, nc), F32), jnp.max(first_max, axis=0, keepdims=True))
    alpha_last, _ = lax.fori_loop(0, n_steps, step, init, unroll=2)
    add_pv(n_steps - 1, alpha_last)


def _mla_flash_kernel(qt_ref, k_ref, vt_ref, o_ref, s_ref, p_ref, m_ref, l_ref, acc_ref):
    _flash_head(qt_ref, k_ref, vt_ref, s_ref, p_ref, m_ref, l_ref, acc_ref)
    nq, _, nc = qt_ref.shape

    def finish(blk, _):
        o = acc_ref[blk] * (1.0 / l_ref[blk])
        o_ref[pl.ds(pl.multiple_of(blk * nc, nc), nc), :] = o.T.astype(BF16)
        return 0

    lax.fori_loop(0, nq, finish, 0, unroll=8)


def _diff_flash_kernel(qt_ref, k_ref, vt_ref, lq_ref, lk_ref, gsub_ref, o_ref,
                       s_ref, p_ref, m_ref, l_ref, acc_ref, *, lam_init):
    _flash_head(qt_ref, k_ref, vt_ref, s_ref, p_ref, m_ref, l_ref, acc_ref)
    nq, _, nc = qt_ref.shape
    bq = nc // 2
    lam_dot = jnp.sum(lq_ref[...] * lk_ref[...], axis=-1, keepdims=True)
    lam_exp = jnp.exp(lam_dot)
    lam = lam_exp[0:1, :] - lam_exp[1:2, :] + lam_init

    def finish(blk, _):
        on = acc_ref[blk] * (1.0 / l_ref[blk])
        a = on[:, :bq] - lam * on[:, bq:]
        ms = jnp.mean(a * a, axis=0, keepdims=True)
        y = (a * lax.rsqrt(ms + RMS_EPS)).T * gsub_ref[...]
        o_ref[pl.ds(pl.multiple_of(blk * bq, bq), bq), :] = (y * (1.0 - lam_init)).astype(BF16)
        return 0

    lax.fori_loop(0, nq, finish, 0, unroll=8)


def _flash_call(kernel, qt, k, vt, extra, batch, seq, dk, dv, nc, key_chunk, name):
    heads = qt.shape[0]
    nq = qt.shape[1] // batch
    n_chunks = seq // key_chunk
    in_bytes = 2 * (nq * dk * nc + seq * dk + seq * dv)
    other_bytes = key_chunk * nc * (4 + 2) + nq * (dv + 2) * nc * 4 + 2 * seq * dv * 2
    once = None if 2 * in_bytes + other_bytes <= VMEM_LIMIT else pl.Buffered(1)
    small = lambda shape: pl.BlockSpec(shape, lambda b, h: (0, 0))
    return pl.pallas_call(
        kernel,
        grid=(batch, heads),
        in_specs=[pl.BlockSpec((None, nq, dk, nc), lambda b, h: (h, b, 0, 0), pipeline_mode=once),
                  pl.BlockSpec((seq, dk), lambda b, h: (b, h), pipeline_mode=once),
                  pl.BlockSpec((None, n_chunks, dv, key_chunk), lambda b, h: (h, b, 0, 0),
                               pipeline_mode=once)] + [small(e.shape) for e in extra],
        out_specs=pl.BlockSpec((seq, dv), lambda b, h: (b, h)),
        out_shape=jax.ShapeDtypeStruct((batch * seq, heads * dv), BF16),
        scratch_shapes=[pltpu.VMEM((key_chunk, nc), F32), pltpu.VMEM((key_chunk, nc), BF16),
                        pltpu.VMEM((nq, 1, nc), F32), pltpu.VMEM((nq, 1, nc), F32),
                        pltpu.VMEM((nq, dv, nc), F32)],
        compiler_params=_params(("parallel", "parallel")),
        name=name,
    )(qt, k, vt, *extra)


def _mla_flash(qt, k, vt, batch, seq):
    return _flash_call(_mla_flash_kernel, qt, k, vt, (), batch, seq, HEAD_PAD, MLA_V, MLA_BQ,
                       MLA_KEY_CHUNK, "mla_flash")


def _diff_flash(qt, k, vt, lam_q, lam_k, g_sub, batch, seq, lam_init):
    return _flash_call(functools.partial(_diff_flash_kernel, lam_init=lam_init), qt, k, vt,
                       (lam_q, lam_k, g_sub), batch, seq, LANES, DIFF_V, 2 * DIFF_BQ, DIFF_KEY_CHUNK, "diff_flash")


def _merge_kernel(om_ref, od_ref, gm_ref, gd_ref, wm_ref, wd_ref, o_ref):
    a = jnp.dot(om_ref[...], wm_ref[...], preferred_element_type=F32)
    b = jnp.dot(od_ref[...], wd_ref[...], preferred_element_type=F32)
    o_ref[...] = (gm_ref[...] * a + gd_ref[...] * b).astype(BF16)


def _outproj_ln_kernel(x_ref, m_ref, w_ref, g_ref, b_ref, o_ref):
    h = jnp.dot(m_ref[...], w_ref[...], preferred_element_type=F32)
    o_ref[...] = _layernorm_rows(DN_ALPHA * x_ref[...] + h, g_ref[...], b_ref[...])


def _ffn_ln_kernel(x_ref, w1_ref, w2_ref, g_ref, b_ref, o_ref, xb_ref, acc_ref):
    j = pl.program_id(1)

    @pl.when(j == 0)
    def _():
        xb_ref[...] = x_ref[...].astype(BF16)
        acc_ref[...] = jnp.zeros_like(acc_ref)

    h = jnp.maximum(jnp.dot(xb_ref[...], w1_ref[...], preferred_element_type=F32), 0.0)
    acc_ref[...] += jnp.dot((h * h).astype(BF16), w2_ref[...], preferred_element_type=F32)

    @pl.when(j == pl.num_programs(1) - 1)
    def _():
        o_ref[...] = _layernorm_rows(DN_ALPHA * x_ref[...] + acc_ref[...], g_ref[...], b_ref[...])


def _merge(o_mla, o_diff, gates, w_br_mla, w_br_diff):
    t, tm = o_mla.shape[0], MIX_TM
    return pl.pallas_call(
        _merge_kernel,
        grid=(t // tm,),
        in_specs=[_row_spec(tm, o_mla.shape[1]), _row_spec(tm, o_diff.shape[1]),
                  pl.BlockSpec((tm, D_MODEL), lambda i: (i, 0)),
                  pl.BlockSpec((tm, D_MODEL), lambda i: (i, 1)),
                  _full_spec(w_br_mla.shape), _full_spec(w_br_diff.shape)],
        out_specs=_row_spec(tm, D_MODEL),
        out_shape=jax.ShapeDtypeStruct((t, D_MODEL), BF16),
        compiler_params=_params(("parallel",)),
        name="branch_merge",
    )(o_mla, o_diff, gates, gates, w_br_mla, w_br_diff)


def _outproj_ln(x, merged, w_out, g, b):
    t, tm = x.shape[0], MIX_TM
    return pl.pallas_call(
        _outproj_ln_kernel,
        grid=(t // tm,),
        in_specs=[_row_spec(tm, D_MODEL), _row_spec(tm, D_MODEL), _full_spec(w_out.shape),
                  _full_spec(g.shape), _full_spec(b.shape)],
        out_specs=_row_spec(tm, D_MODEL),
        out_shape=jax.ShapeDtypeStruct((t, D_MODEL), F32),
        compiler_params=_params(("parallel",)),
        name="outproj_ln",
    )(x, merged, w_out, g, b)


def _ffn_ln(x, w1, w2, g, b):
    t, tm, tf = x.shape[0], MIX_TM, FFN_TF
    return pl.pallas_call(
        _ffn_ln_kernel,
        grid=(t // tm, D_FF // tf),
        in_specs=[pl.BlockSpec((tm, D_MODEL), lambda i, j: (i, 0)),
                  pl.BlockSpec((D_MODEL, tf), lambda i, j: (0, j)),
                  pl.BlockSpec((tf, D_MODEL), lambda i, j: (j, 0)),
                  pl.BlockSpec((1, D_MODEL), lambda i, j: (0, 0)),
                  pl.BlockSpec((1, D_MODEL), lambda i, j: (0, 0))],
        out_specs=pl.BlockSpec((tm, D_MODEL), lambda i, j: (i, 0)),
        out_shape=jax.ShapeDtypeStruct((t, D_MODEL), F32),
        scratch_shapes=[pltpu.VMEM((tm, D_MODEL), BF16), pltpu.VMEM((tm, D_MODEL), F32)],
        compiler_params=_params(("parallel", "arbitrary")),
        name="ffn_ln",
    )(x, w1, w2, g, b)


def _rope_tables(seq, rot_dim, group):
    half = rot_dim // 2
    inv_freq = ROPE_THETA ** (-jnp.arange(0, rot_dim, 2, dtype=F32) / rot_dim)
    ang = jnp.arange(seq, dtype=F32)[:, None] * inv_freq[None, :]
    cos, sin = jnp.cos(ang), jnp.sin(ang)
    zeros = lambda n: jnp.zeros((seq, n), F32)
    c = jnp.concatenate([cos, cos, jnp.ones((seq, group - rot_dim), F32)], axis=1)
    s_fwd = jnp.concatenate([-sin, zeros(group - half)], axis=1)
    s_bwd = jnp.concatenate([zeros(half), sin, zeros(group - rot_dim)], axis=1)
    return jnp.stack([c, s_fwd, s_bwd])


def _prep_weights(w_in, b_gate, g_qa, w_qb, g_kva, w_kvb, g_sub, w_br_mla, w_br_diff, w_out,
                  ln1_g, ln1_b, w_ff1, w_ff2, ln2_g, ln2_b, l):
    c0 = C_QA
    c1 = c0 + C_KVA
    c2 = c1 + C_DQ
    c3 = c2 + C_DK
    c4 = c3 + C_DV
    wi = w_in[l]
    qb = w_qb[l].reshape(MLA_Q_LORA, MLA_HEADS, MLA_NOPE + MLA_ROPE)
    qb_nope = qb[:, :, :MLA_NOPE].reshape(MLA_Q_LORA, MLA_HEADS * MLA_NOPE)
    qb_rope = jnp.pad(qb[:, :, MLA_NOPE:], ((0, 0), (0, 0), (0, LANES - MLA_ROPE)))
    qb_rope = qb_rope.reshape(MLA_Q_LORA, MLA_HEADS * LANES)
    kvb = w_kvb[l].reshape(MLA_KV_LORA, MLA_HEADS, MLA_NOPE + MLA_V)
    kvb_k = kvb[:, :, :MLA_NOPE].reshape(MLA_KV_LORA, MLA_HEADS * MLA_NOPE)
    kvb_v = kvb[:, :, MLA_NOPE:].reshape(MLA_KV_LORA, MLA_HEADS * MLA_V)
    row = lambda v: v[l].reshape(1, -1)
    return dict(
        wqa=wi[:, :c0].astype(BF16),
        wkva=jnp.pad(wi[:, c0:c1], ((0, 0), (0, LANES - MLA_ROPE))).astype(BF16),
        wdq=wi[:, c1:c2].astype(BF16),
        wdk=wi[:, c2:c3].astype(BF16),
        wdv=wi[:, c3:c4].astype(BF16),
        wgate=wi[:, c4:].astype(BF16),
        b_gate=row(b_gate),
        g_qa=row(g_qa),
        wqb=jnp.concatenate([qb_nope, qb_rope], axis=1).astype(BF16),
        g_kva=row(g_kva),
        wkvb=jnp.concatenate([kvb_k, kvb_v], axis=1).astype(BF16),
        g_sub=row(g_sub),
        w_br_mla=w_br_mla[l].astype(BF16),
        w_br_diff=w_br_diff[l].astype(BF16),
        w_out=w_out[l].astype(BF16),
        ln1_g=row(ln1_g), ln1_b=row(ln1_b),
        w_ff1=w_ff1[l].astype(BF16), w_ff2=w_ff2[l].astype(BF16),
        ln2_g=row(ln2_g), ln2_b=row(ln2_b),
    )


def _layer(x3, w, tab_mla, tab_diff, lam_q, lam_k, lam_init):
    batch, seq, _ = x3.shape
    x = x3.reshape(batch * seq, D_MODEL)

    qt, k, vt = _mla_proj(x, w["wqa"], w["g_qa"], w["wqb"], w["wkva"], w["g_kva"], w["wkvb"], tab_mla, seq)
    o_mla = _mla_flash(qt, k, vt, batch, seq)

    dqt, dk, dvt = _diff_proj(x, w["wdq"], w["wdk"], w["wdv"], tab_diff, seq)
    o_diff = _diff_flash(dqt, dk, dvt, lam_q, lam_k, w["g_sub"], batch, seq, lam_init)

    gates = _gates(x, w["wgate"], w["b_gate"])
    merged = _merge(o_mla, o_diff, gates, w["w_br_mla"], w["w_br_diff"])
    x1 = _outproj_ln(x, merged, w["w_out"], w["ln1_g"], w["ln1_b"])
    y = _ffn_ln(x1, w["w_ff1"], w["w_ff2"], w["ln2_g"], w["ln2_b"])
    return y.reshape(batch, seq, D_MODEL)


def kernel(x_prompt, x_sample, w_in, b_gate, g_qa, w_qb, g_kva, w_kvb, lam_q, lam_k, g_sub,
           w_br_mla, w_br_diff, w_out, ln1_g, ln1_b, w_ff1, w_ff2, ln2_g, ln2_b):
    outs = [x_prompt, x_sample]
    max_seq = max(x3.shape[1] for x3 in outs)
    tab_mla = jnp.pad(_rope_tables(max_seq, MLA_ROPE, MLA_ROPE), ((0, 0), (0, 0), (0, LANES - MLA_ROPE)))
    tab_diff = jnp.tile(_rope_tables(max_seq, DIFF_ROT, DIFF_QK), (1, 1, LANES // DIFF_QK))
    for l in range(DEPTH):
        lam_init = 0.8 - 0.6 * math.exp(-0.3 * l)
        w = _prep_weights(w_in, b_gate, g_qa, w_qb, g_kva, w_kvb, g_sub, w_br_mla, w_br_diff, w_out,
                          ln1_g, ln1_b, w_ff1, w_ff2, ln2_g, ln2_b, l)
        outs = [_layer(x3, w, tab_mla, tab_diff, lam_q[l], lam_k[l], lam_init) for x3 in outs]
    return tuple(outs)
```
